```python
import math
import jax
import jax.numpy as jnp
from jax import lax
import numpy as np

D_MODEL = 1024
BATCH = 2
SEQ = 16384
DEPTH = 2

EXPAND = 2
D_INNER = EXPAND * D_MODEL
CHUNK = 128
NORM_EPS = 1e-6
S5_WIDTH = D_INNER // 2
S5_GROUP = 16
S5_GROUPS = S5_WIDTH // S5_GROUP
S5_STATE = 64
SSD_WIDTH = D_INNER - S5_WIDTH
SSD_HEAD_DIM = 64
SSD_HEADS = SSD_WIDTH // SSD_HEAD_DIM
SSD_GROUPS = 2
SSD_REP = SSD_HEADS // SSD_GROUPS
SSD_STATE = 128
SSD_CONV = 4
SSD_CONV_DIM = SSD_WIDTH + 2 * SSD_GROUPS * SSD_STATE
AB_PROJ = D_INNER + S5_WIDTH + SSD_CONV_DIM + SSD_HEADS
RET_HEADS = 4
RET_QK_DIM = D_MODEL // RET_HEADS
RET_V_DIM = D_INNER // RET_HEADS
RET_PROJ = 2 * RET_HEADS * RET_QK_DIM + 2 * D_INNER
ROPE_BASE = 10000.0
N_EVEN = (DEPTH + 1) // 2
N_ODD = DEPTH // 2

kernel_name = 's5_ssd_retention_hybrid'

F32 = jnp.float32


def rmsnorm(x, w, n_groups=1):
    xf = x.astype(F32)
    xg = xf.reshape(*x.shape[:-1], n_groups, x.shape[-1] // n_groups)
    xg = xg * lax.rsqrt(jnp.mean(xg * xg, axis=-1, keepdims=True) + NORM_EPS)
    return (xg.reshape(x.shape) * w.astype(F32)).astype(x.dtype)


def _complex_affine_combine(left, right):
    a1r, a1i, b1r, b1i = left
    a2r, a2i, b2r, b2i = right
    return (a2r * a1r - a2i * a1i,
            a2r * a1i + a2i * a1r,
            a2r * b1r - a2i * b1i + b2r,
            a2r * b1i + a2i * b1r + b2i)


def s5_mixer(u, lam_re, lam_im, log_dt, b_re, b_im, c_re, c_im, d_skip, glu_w, glu_b):
    bsz, seqlen, _ = u.shape
    n_chunks = seqlen // CHUNK
    uf = u.astype(F32)
    dt = jnp.exp(log_dt.astype(F32))[:, None]
    lr = jnp.minimum(lam_re.astype(F32), -1e-4)
    li = lam_im.astype(F32)
    mag = jnp.exp(lr * dt)
    ab_re = mag * jnp.cos(li * dt)
    ab_im = mag * jnp.sin(li * dt)
    den = lr * lr + li * li
    nr = ab_re - 1.0
    coef_re = (nr * lr + ab_im * li) / den
    coef_im = (ab_im * lr - nr * li) / den
    br = b_re.astype(F32)
    bi = b_im.astype(F32)
    bb_re = coef_re[..., None] * br - coef_im[..., None] * bi
    bb_im = coef_re[..., None] * bi + coef_im[..., None] * br
    cr = c_re.astype(F32)
    ci = c_im.astype(F32)
    st_shape = (bsz, CHUNK, S5_GROUPS, S5_STATE)
    a_re = jnp.broadcast_to(ab_re, st_shape)
    a_im = jnp.broadcast_to(ab_im, st_shape)
    u_chunks = uf.reshape(bsz, n_chunks, CHUNK, S5_GROUPS, S5_GROUP).transpose(1, 0, 2, 3, 4)

    def step(carry, u_c):
        h_re, h_im = carry
        bu_re = jnp.einsum('btgh,gph->btgp', u_c, bb_re)
        bu_im = jnp.einsum('btgh,gph->btgp', u_c, bb_im)
        acum_re, acum_im, x_re, x_im = lax.associative_scan(
            _complex_affine_combine, (a_re, a_im, bu_re, bu_im), axis=1)
        x_re = x_re + acum_re * h_re[:, None] - acum_im * h_im[:, None]
        x_im = x_im + acum_re * h_im[:, None] + acum_im * h_re[:, None]
        y = jnp.einsum('btgp,ghp->btgh', x_re, cr) - jnp.einsum('btgp,ghp->btgh', x_im, ci)
        return (x_re[:, -1], x_im[:, -1]), y

    zero = jnp.zeros((bsz, S5_GROUPS, S5_STATE), F32)
    _, ys = lax.scan(step, (zero, zero), u_chunks)
    y = ys.transpose(1, 0, 2, 3, 4).reshape(bsz, seqlen, S5_WIDTH)
    y = jax.nn.gelu(y + d_skip.astype(F32) * uf)
    return y * jax.nn.sigmoid(y @ glu_w.astype(F32) + glu_b.astype(F32))


def causal_depthwise_conv(x, w, b):
    k = w.shape[0]
    y = lax.conv_general_dilated(x, w[:, None, :], window_strides=(1,), padding=[(k - 1, 0)],
                                 dimension_numbers=('NWC', 'WIO', 'NWC'),
                                 feature_group_count=x.shape[-1])
    return y + b


def ssd_mixer(xbc, dt_raw, conv_w, conv_b, dt_bias, a_log, d_skip):
    bsz, seqlen, _ = xbc.shape
    nc = seqlen // CHUNK
    xbc = jax.nn.silu(causal_depthwise_conv(xbc.astype(F32), conv_w.astype(F32), conv_b.astype(F32)))
    gn = SSD_GROUPS * SSD_STATE
    xs = xbc[..., :SSD_WIDTH].reshape(bsz, nc, CHUNK, SSD_GROUPS, SSD_REP, SSD_HEAD_DIM)
    bm = xbc[..., SSD_WIDTH:SSD_WIDTH + gn].reshape(bsz, nc, CHUNK, SSD_GROUPS, SSD_STATE)
    cm = xbc[..., SSD_WIDTH + gn:].reshape(bsz, nc, CHUNK, SSD_GROUPS, SSD_STATE)
    dt = jax.nn.softplus(dt_raw.astype(F32) + dt_bias.astype(F32))
    a = -jnp.exp(a_log.astype(F32))
    xdt = xs * dt.reshape(bsz, nc, CHUNK, SSD_GROUPS, SSD_REP)[..., None]
    da = (dt * a).reshape(bsz, nc, CHUNK, SSD_HEADS).transpose(0, 3, 1, 2)
    a_cs = jnp.cumsum(da, axis=-1)
    causal = jnp.tril(jnp.ones((CHUNK, CHUNK), dtype=bool))
    seg = a_cs[..., :, None] - a_cs[..., None, :]
    decay_mat = jnp.exp(jnp.where(causal, seg, -jnp.inf)).reshape(
        bsz, SSD_GROUPS, SSD_REP, nc, CHUNK, CHUNK)
    cb = jnp.einsum('bclgn,bcsgn->bgcls', cm, bm)
    y_diag = jnp.einsum('bgrcls,bcsgrp->bclgrp', cb[:, :, None] * decay_mat, xdt)

    def to_bclgr(t):
        return t.reshape(bsz, SSD_GROUPS, SSD_REP, nc, CHUNK).transpose(0, 3, 4, 1, 2)

    decay_states = to_bclgr(jnp.exp(a_cs[..., -1:] - a_cs))
    states = jnp.einsum('bclgn,bclgrp->bcgrpn', bm, xdt * decay_states[..., None])
    chunk_decay = jnp.exp(a_cs[..., -1]).reshape(bsz, SSD_GROUPS, SSD_REP, nc).transpose(3, 0, 1, 2)

    def step(h, inp):
        dec, st = inp
        return h * dec[..., None, None] + st, h

    init = jnp.zeros((bsz, SSD_GROUPS, SSD_REP, SSD_HEAD_DIM, SSD_STATE), F32)
    _, prev = lax.scan(step, init, (chunk_decay, states.transpose(1, 0, 2, 3, 4, 5)))
    prev = prev.transpose(1, 0, 2, 3, 4, 5)
    y_off = jnp.einsum('bclgn,bcgrpn->bclgrp', cm, prev) * to_bclgr(jnp.exp(a_cs))[..., None]
    y = y_diag + y_off + xs * d_skip.astype(F32).reshape(SSD_GROUPS, SSD_REP)[:, :, None]
    return y.reshape(bsz, seqlen, SSD_WIDTH)


def s5_ssd_layer(h, w_in, lam_re, lam_im, log_dt, b_re, b_im, c_re, c_im, s5_d, glu_w, glu_b,
                 conv_w, conv_b, dt_bias, a_log, ssd_d, ssd_norm_w, w_out):
    p = h @ w_in
    o1 = D_INNER
    o2 = o1 + S5_WIDTH
    o3 = o2 + SSD_CONV_DIM
    z = p[..., :o1].astype(F32)
    y_a = s5_mixer(p[..., o1:o2], lam_re, lam_im, log_dt, b_re, b_im, c_re, c_im, s5_d, glu_w, glu_b)
    y_a = y_a * jax.nn.silu(z[..., :S5_WIDTH])
    y_b = ssd_mixer(p[..., o2:o3], p[..., o3:], conv_w, conv_b, dt_bias, a_log, ssd_d)
    y_b = rmsnorm(y_b * jax.nn.silu(z[..., S5_WIDTH:]), ssd_norm_w, SSD_GROUPS)
    y = jnp.concatenate([y_a, y_b], axis=-1)
    return y.astype(w_out.dtype) @ w_out


def rotary(x, cos, sin):
    x1 = x[..., ::2]
    x2 = x[..., 1::2]
    rot = jnp.stack([-x2, x1], axis=-1).reshape(x.shape)
    return x * cos[:, None] + rot * sin[:, None]


def chunkwise_retention(q, k, v):
    bsz, seqlen = q.shape[0], q.shape[1]
    nc = seqlen // CHUNK
    log_gamma = jnp.log(1.0 - 2.0 ** (-5.0 - jnp.arange(RET_HEADS, dtype=F32)))
    pos = jnp.arange(CHUNK, dtype=F32)
    rel = pos[:, None] - pos[None, :]
    intra = jnp.where(rel >= 0, jnp.exp(log_gamma[:, None, None] * jnp.maximum(rel, 0.0)), 0.0)
    qc = q.reshape(bsz, nc, CHUNK, RET_HEADS, RET_QK_DIM)
    kc = k.reshape(bsz, nc, CHUNK, RET_HEADS, RET_QK_DIM)
    vc = v.reshape(bsz, nc, CHUNK, RET_HEADS, RET_V_DIM)
    scores = jnp.einsum('bclhd,bcshd->bchls', qc, kc) * intra
    inner = jnp.einsum('bchls,bcshv->bclhv', scores, vc)
    q_decay = jnp.exp(log_gamma[None, :] * (pos[:, None] + 1.0))
    k_decay = jnp.exp(log_gamma[None, :] * (CHUNK - 1.0 - pos)[:, None])
    chunk_decay = jnp.exp(log_gamma * CHUNK)
    qd = (qc * q_decay[..., None]).transpose(1, 0, 2, 3, 4)
    kd = (kc * k_decay[..., None]).transpose(1, 0, 2, 3, 4)
    vt = vc.transpose(1, 0, 2, 3, 4)

    def step(state, inp):
        q_t, k_t, v_t = inp
        cross = jnp.einsum('blhd,bhdv->blhv', q_t, state)
        state = state * chunk_decay[:, None, None] + jnp.einsum('blhd,blhv->bhdv', k_t, v_t)
        return state, cross

    init = jnp.zeros((bsz, RET_HEADS, RET_QK_DIM, RET_V_DIM), F32)
    _, cross = lax.scan(step, init, (qd, kd, vt))
    o = inner + cross.transpose(1, 0, 2, 3, 4)
    return o.reshape(bsz, seqlen, RET_HEADS, RET_V_DIM)


def retention_layer(h, w_in, gn_w, gn_b, w_out):
    bsz, seqlen, _ = h.shape
    p = (h @ w_in).astype(F32)
    qk_w = RET_HEADS * RET_QK_DIM
    q = p[..., :qk_w].reshape(bsz, seqlen, RET_HEADS, RET_QK_DIM)
    k = p[..., qk_w:2 * qk_w].reshape(bsz, seqlen, RET_HEADS, RET_QK_DIM)
    v = p[..., 2 * qk_w:2 * qk_w + D_INNER].reshape(bsz, seqlen, RET_HEADS, RET_V_DIM)
    g = p[..., 2 * qk_w + D_INNER:]
    pos = jnp.arange(seqlen, dtype=F32)
    angle = jnp.repeat(1.0 / (ROPE_BASE ** jnp.linspace(0.0, 1.0, RET_QK_DIM // 2, dtype=F32)), 2)
    theta = pos[:, None] * angle[None, :]
    cos = jnp.cos(theta)
    sin = jnp.sin(theta)
    q = rotary(q, cos, sin)
    k = rotary(k, cos, sin) * (RET_QK_DIM ** -0.5)
    o = chunkwise_retention(q, k, v)
    mu = jnp.mean(o, axis=-1, keepdims=True)
    var = jnp.mean(jnp.square(o - mu), axis=-1, keepdims=True)
    o = ((o - mu) * lax.rsqrt(var + NORM_EPS)).reshape(bsz, seqlen, D_INNER)
    o = o * gn_w.astype(F32) + gn_b.astype(F32)
    y = jax.nn.silu(g) * o
    return y.astype(w_out.dtype) @ w_out


def setup_inputs(seed: int = 0) -> dict:
    key = jax.random.key(seed)
    ks = jax.random.split(key, 25)

    def nrm(k, shape, s):
        return jax.random.normal(k, shape, F32) * s

    dt_ssd = jnp.exp(jax.random.uniform(ks[15], (N_EVEN, SSD_HEADS), F32,
                                        minval=math.log(1e-3), maxval=math.log(1e-1)))
    return {
        'x': nrm(ks[0], (BATCH, SEQ, D_MODEL), 1.0),
        'layer_norm_w': 1.0 + nrm(ks[1], (DEPTH, D_MODEL), 0.01),
        'ab_w_in': nrm(ks[2], (N_EVEN, D_MODEL, AB_PROJ), D_MODEL ** -0.5),
        's5_lam_re': -0.5 + nrm(ks[3], (N_EVEN, S5_GROUPS, S5_STATE), 0.01),
        's5_lam_im': math.pi * jnp.arange(S5_STATE, dtype=F32) + nrm(ks[4], (N_EVEN, S5_GROUPS, S5_STATE), 0.01),
        's5_log_dt': jax.random.uniform(ks[5], (N_EVEN, S5_GROUPS), F32,
                                        minval=math.log(1e-3), maxval=math.log(1e-1)),
        's5_b_re': nrm(ks[6], (N_EVEN, S5_GROUPS, S5_STATE, S5_GROUP), (2 * S5_GROUP) ** -0.5),
        's5_b_im': nrm(ks[7], (N_EVEN, S5_GROUPS, S5_STATE, S5_GROUP), (2 * S5_GROUP) ** -0.5),
        's5_c_re': nrm(ks[8], (N_EVEN, S5_GROUPS, S5_GROUP, S5_STATE), S5_STATE ** -0.5),
        's5_c_im': nrm(ks[9], (N_EVEN, S5_GROUPS, S5_GROUP, S5_STATE), S5_STATE ** -0.5),
        's5_d': nrm(ks[10], (N_EVEN, S5_WIDTH), 1.0),
        's5_glu_w': nrm(ks[11], (N_EVEN, S5_WIDTH, S5_WIDTH), S5_WIDTH ** -0.5),
        's5_glu_b': nrm(ks[12], (N_EVEN, S5_WIDTH), 0.01),
        'ssd_conv_w': nrm(ks[13], (N_EVEN, SSD_CONV, SSD_CONV_DIM), SSD_CONV ** -0.5),
        'ssd_conv_b': nrm(ks[14], (N_EVEN, SSD_CONV_DIM), 0.01),
        'ssd_dt_bias': dt_ssd + jnp.log(-jnp.expm1(-dt_ssd)),
        'ssd_a_log': jnp.log(jax.random.uniform(ks[16], (N_EVEN, SSD_HEADS), F32, minval=1.0, maxval=16.0)),
        'ssd_d': 1.0 + nrm(ks[17], (N_EVEN, SSD_HEADS), 0.1),
        'ssd_norm_w': 1.0 + nrm(ks[18], (N_EVEN, SSD_WIDTH), 0.01),
        'ab_w_out': nrm(ks[19], (N_EVEN, D_INNER, D_MODEL), D_INNER ** -0.5),
        'ret_w_in': nrm(ks[20], (N_ODD, D_MODEL, RET_PROJ), D_MODEL ** -0.5),
        'ret_gn_w': 1.0 + nrm(ks[21], (N_ODD, D_INNER), 0.01),
        'ret_gn_b': nrm(ks[22], (N_ODD, D_INNER), 0.01),
        'ret_w_out': nrm(ks[23], (N_ODD, D_INNER, D_MODEL), D_INNER ** -0.5),
        'final_norm_w': 1.0 + nrm(ks[24], (D_MODEL,), 0.01),
    }


def reference(x, layer_norm_w, ab_w_in, s5_lam_re, s5_lam_im, s5_log_dt, s5_b_re, s5_b_im,
              s5_c_re, s5_c_im, s5_d, s5_glu_w, s5_glu_b, ssd_conv_w, ssd_conv_b, ssd_dt_bias,
              ssd_a_log, ssd_d, ssd_norm_w, ab_w_out, ret_w_in, ret_gn_w, ret_gn_b, ret_w_out,
              final_norm_w):
    h = x
    for i in range(DEPTH):
        hn = rmsnorm(h, layer_norm_w[i])
        j = i // 2
        if i % 2 == 0:
            y = s5_ssd_layer(hn, ab_w_in[j], s5_lam_re[j], s5_lam_im[j], s5_log_dt[j],
                             s5_b_re[j], s5_b_im[j], s5_c_re[j], s5_c_im[j], s5_d[j],
                             s5_glu_w[j], s5_glu_b[j], ssd_conv_w[j], ssd_conv_b[j],
                             ssd_dt_bias[j], ssd_a_log[j], ssd_d[j], ssd_norm_w[j], ab_w_out[j])
        else:
            y = retention_layer(hn, ret_w_in[j], ret_gn_w[j], ret_gn_b[j], ret_w_out[j])
        h = h + y.astype(h.dtype)
    return rmsnorm(h, final_norm_w)
```

```python
import functools
import math

import jax
import jax.numpy as jnp
from jax import lax
from jax.experimental import pallas as pl
from jax.experimental.pallas import tpu as pltpu

F32 = jnp.float32
BF16 = jnp.bfloat16

D_MODEL = 1024
D_INNER = 2048
CHUNK = 128
NORM_EPS = 1e-6
S5_WIDTH = 1024
S5_GROUP = 16
S5_GROUPS = 64
S5_STATE = 64
SSD_WIDTH = 1024
SSD_HEAD_DIM = 64
SSD_HEADS = 16
SSD_GROUPS = 2
SSD_REP = 8
SSD_STATE = 128
SSD_CONV = 4
SSD_CONV_DIM = 1536
RET_HEADS = 4
RET_QK_DIM = 256
RET_V_DIM = 512
ROPE_BASE = 10000.0

LANES = 128
SUBLANES = 8
VMEM_CAP = 60 * 1024 * 1024
TOKEN_TILE = 512
RET_CHUNK = 128


def _params(semantics, block_bytes, scratch_bytes=0):
    need = 2 * block_bytes + scratch_bytes + 16 * 1024 * 1024
    return pltpu.CompilerParams(dimension_semantics=semantics,
                                vmem_limit_bytes=int(min(need, VMEM_CAP)))


def _nbytes(shape, dtype):
    return math.prod(shape) * jnp.dtype(dtype).itemsize


def _silu(x):
    return x * jax.nn.sigmoid(x)


def _split3(x):
    hi = x.astype(BF16)
    r1 = x - hi.astype(F32)
    mid = r1.astype(BF16)
    lo = (r1 - mid.astype(F32)).astype(BF16)
    return hi, mid, lo


def _dot(a, b):
    return jnp.dot(a, b, preferred_element_type=F32)


def _dot_nt(a, b):
    return lax.dot_general(a, b, (((1,), (1,)), ((), ())), preferred_element_type=F32)


def _norm_proj_kernel(*refs, n_out, n_rot, k_scale):
    x_ref, nw_ref = refs[0], refs[1]
    pos = 2
    if n_rot:
        cos_ref, sin_ref = refs[2], refs[3]
        pos = 4
    w_refs = refs[pos:pos + n_out]
    o_refs = refs[pos + n_out:pos + 2 * n_out]
    x = x_ref[...]
    ms = jnp.mean(x * x, axis=-1, keepdims=True)
    xn = (x * lax.rsqrt(ms + NORM_EPS) * nw_ref[...]).astype(BF16)
    for idx in range(n_out):
        acc = _dot(xn, w_refs[idx][...])
        o_ref = o_refs[idx]
        if idx < n_rot:
            cos = cos_ref[...]
            sin = sin_ref[...]
            scale = k_scale if idx == 1 else 1.0
            half = RET_QK_DIM // 2
            for h in range(RET_HEADS):
                x1 = acc[:, h * RET_QK_DIM:h * RET_QK_DIM + half]
                x2 = acc[:, h * RET_QK_DIM + half:(h + 1) * RET_QK_DIM]
                o_ref[:, h * RET_QK_DIM:h * RET_QK_DIM + half] = (
                    (x1 * cos - x2 * sin) * scale).astype(o_ref.dtype)
                o_ref[:, h * RET_QK_DIM + half:(h + 1) * RET_QK_DIM] = (
                    (x2 * cos + x1 * sin) * scale).astype(o_ref.dtype)
        else:
            o_ref[...] = acc.astype(o_ref.dtype)


def _norm_proj(x2d, norm_w, weights, out_dtypes, rot=None, seq_len=None):
    n, d = x2d.shape
    tm = min(TOKEN_TILE, n)
    n_out = len(weights)
    in_specs = [pl.BlockSpec((tm, d), lambda i: (i, 0)),
                pl.BlockSpec((1, d), lambda i: (0, 0))]
    args = [x2d, norm_w.reshape(1, d)]
    blk = _nbytes((tm, d), F32)
    n_rot = 0
    if rot is not None:
        cos, sin = rot
        tiles_per_seq = seq_len // tm
        in_specs += [pl.BlockSpec((tm, cos.shape[1]), lambda i: (i % tiles_per_seq, 0))] * 2
        args += [cos, sin]
        blk += 2 * _nbytes((tm, cos.shape[1]), F32)
        n_rot = 2
    for w in weights:
        in_specs.append(pl.BlockSpec(w.shape, lambda i: (0, 0)))
        args.append(w)
        blk += _nbytes(w.shape, w.dtype)
    out_specs = []
    out_shape = []
    for w, dt in zip(weights, out_dtypes):
        out_specs.append(pl.BlockSpec((tm, w.shape[1]), lambda i: (i, 0)))
        out_shape.append(jax.ShapeDtypeStruct((n, w.shape[1]), dt))
        blk += _nbytes((tm, w.shape[1]), dt)
    kern = functools.partial(_norm_proj_kernel, n_out=n_out, n_rot=n_rot,
                             k_scale=RET_QK_DIM ** -0.5)
    return pl.pallas_call(
        kern, grid=(n // tm,), in_specs=in_specs, out_specs=out_specs, out_shape=out_shape,
        compiler_params=_params(("parallel",), blk), name="norm_proj")(*args)


def _s5_taps_kernel(r_ref, a_ref, k_ref):
    k_ref[...] = jnp.dot(r_ref[...], a_ref[...], preferred_element_type=F32,
                         precision=lax.Precision.HIGHEST)


def _s5_taps(rcat, apcat):
    g, m, kk = rcat.shape
    t = apcat.shape[2]
    return pl.pallas_call(
        _s5_taps_kernel, grid=(g,),
        in_specs=[pl.BlockSpec((None, m, kk), lambda i: (i, 0, 0)),
                  pl.BlockSpec((None, kk, t), lambda i: (i, 0, 0))],
        out_specs=pl.BlockSpec((None, m, t), lambda i: (i, 0, 0)),
        out_shape=jax.ShapeDtypeStruct((g, m, t), F32),
        compiler_params=_params(("parallel",), _nbytes((m, kk + t), F32) + _nbytes((kk, t), F32)),
        name="s5_taps")(rcat, apcat)


def _s5_core_kernel(u_ref, k_ref, p_ref, q_ref, sc_ref, y_ref, toep_ref, *, n_chunks):
    t = CHUNK
    h = S5_GROUP
    rows = u_ref.shape[0]
    s_idx = lax.broadcasted_iota(jnp.int32, (t, t), 0)
    t_idx = lax.broadcasted_iota(jnp.int32, (t, t), 1)
    causal = t_idx >= s_idx

    def build(hi, carry):
        kblk = k_ref[pl.ds(pl.multiple_of(hi * h, h), h), :]
        r0 = pl.multiple_of(hi * t, t)
        for ho in range(h):
            kb = jnp.broadcast_to(kblk[ho:ho + 1, :], (t, t))
            tz = pltpu.roll(kb, 0, 1, stride=1, stride_axis=0)
            toep_ref[pl.ds(r0, t), ho * t:(ho + 1) * t] = jnp.where(causal, tz, 0.0).astype(BF16)
        return carry

    lax.fori_loop(0, h, build, 0)

    u = u_ref[...]
    e = _dot(u, p_ref[...])
    half = S5_STATE
    row_idx = lax.broadcasted_iota(jnp.int32, (n_chunks, 2 * half), 0)
    hprev = []
    for b in range(rows // n_chunks):
        acc = e[b * n_chunks:(b + 1) * n_chunks, :]
        d = 1
        lvl = 0
        while d < n_chunks:
            sh = jnp.where(row_idx >= d, pltpu.roll(acc, d, 0), 0.0)
            c1 = sc_ref[2 * lvl:2 * lvl + 1, :]
            c2 = sc_ref[2 * lvl + 1:2 * lvl + 2, :]
            acc = acc + c1 * sh + c2 * pltpu.roll(sh, half, 1)
            d *= 2
            lvl += 1
        hprev.append(jnp.where(row_idx >= 1, pltpu.roll(acc, 1, 0), 0.0))
    hp = jnp.concatenate(hprev, axis=0).astype(BF16)
    y = _dot(u, toep_ref[...]) + _dot(hp, q_ref[...])
    y_ref[...] = y.astype(y_ref.dtype)


def _s5_core(u2, taps, pmat, qmat, sc, n_chunks):
    rows = u2.shape[0]
    g = taps.shape[0]
    w = S5_GROUP * CHUNK
    blk = (2 * _nbytes((rows, w), BF16) + _nbytes(taps.shape[1:], F32)
           + _nbytes(pmat.shape[1:], BF16) + _nbytes(qmat.shape[1:], BF16) + _nbytes(sc.shape[1:], F32))
    return pl.pallas_call(
        functools.partial(_s5_core_kernel, n_chunks=n_chunks), grid=(g,),
        in_specs=[pl.BlockSpec((rows, w), lambda i: (0, i)),
                  pl.BlockSpec((None,) + taps.shape[1:], lambda i: (i, 0, 0)),
                  pl.BlockSpec((None,) + pmat.shape[1:], lambda i: (i, 0, 0)),
                  pl.BlockSpec((None,) + qmat.shape[1:], lambda i: (i, 0, 0)),
                  pl.BlockSpec((None,) + sc.shape[1:], lambda i: (i, 0, 0))],
        out_specs=pl.BlockSpec((rows, w), lambda i: (0, i)),
        out_shape=jax.ShapeDtypeStruct((rows, g * w), BF16),
        scratch_shapes=[pltpu.VMEM((w, w), BF16)],
        compiler_params=_params(("parallel",), blk, _nbytes((w, w), BF16)),
        name="s5_core")(u2, taps, pmat, qmat, sc)


def _s5_prepare(lam_re, lam_im, log_dt, b_re, b_im, c_re, c_im, n_chunks):
    t = CHUNK
    dt = jnp.exp(log_dt.astype(F32))[:, None]
    lr = jnp.minimum(lam_re.astype(F32), -1e-4)
    li = lam_im.astype(F32)
    mag = jnp.exp(lr * dt)
    ab_re = mag * jnp.cos(li * dt)
    ab_im = mag * jnp.sin(li * dt)
    den = lr * lr + li * li
    nr = ab_re - 1.0
    coef_re = (nr * lr + ab_im * li) / den
    coef_im = (ab_im * lr - nr * li) / den
    br = b_re.astype(F32)
    bi = b_im.astype(F32)
    bb_re = coef_re[..., None] * br - coef_im[..., None] * bi
    bb_im = coef_re[..., None] * bi + coef_im[..., None] * br
    cr = c_re.astype(F32)
    ci = c_im.astype(F32)
    j = jnp.arange(t + 1, dtype=F32)
    pmag = jnp.exp((lr * dt)[..., None] * j)
    ang = (li * dt)[..., None] * j
    pw_re = pmag * jnp.cos(ang)
    pw_im = pmag * jnp.sin(ang)
    bbt_re = bb_re.transpose(0, 2, 1)
    bbt_im = bb_im.transpose(0, 2, 1)
    g = lr.shape[0]
    r_re = cr[:, None] * bbt_re[:, :, None] - ci[:, None] * bbt_im[:, :, None]
    r_im = cr[:, None] * bbt_im[:, :, None] + ci[:, None] * bbt_re[:, :, None]
    rcat = jnp.concatenate([r_re, -r_im], axis=-1).reshape(g, S5_GROUP * S5_GROUP, 2 * S5_STATE)
    apcat = jnp.concatenate([pw_re[..., :t], pw_im[..., :t]], axis=1)
    fl_re = pw_re[..., t - 1::-1].transpose(0, 2, 1)[:, None]
    fl_im = pw_im[..., t - 1::-1].transpose(0, 2, 1)[:, None]
    p_re = fl_re * bbt_re[:, :, None] - fl_im * bbt_im[:, :, None]
    p_im = fl_re * bbt_im[:, :, None] + fl_im * bbt_re[:, :, None]
    pmat = jnp.concatenate([p_re, p_im], axis=-1).reshape(g, S5_GROUP * t, 2 * S5_STATE).astype(BF16)
    crt = cr.transpose(0, 2, 1)[..., None]
    cit = ci.transpose(0, 2, 1)[..., None]
    w_re = crt * pw_re[:, :, None, 1:] - cit * pw_im[:, :, None, 1:]
    w_im = crt * pw_im[:, :, None, 1:] + cit * pw_re[:, :, None, 1:]
    qmat = jnp.concatenate([w_re, -w_im], axis=1).reshape(g, 2 * S5_STATE, S5_GROUP * t).astype(BF16)
    rows = []
    d = 1
    while d < n_chunks:
        e = float(t * d)
        m = jnp.exp(lr * dt * e)
        a_r = m * jnp.cos(li * dt * e)
        a_i = m * jnp.sin(li * dt * e)
        rows.append(jnp.concatenate([a_r, a_r], axis=-1))
        rows.append(jnp.concatenate([-a_i, a_i], axis=-1))
        d *= 2
    while len(rows) < 2 * SUBLANES:
        rows.append(jnp.zeros((g, 2 * S5_STATE), F32))
    sc = jnp.stack(rows, axis=1)
    return rcat, apcat, pmat, qmat, sc


def _ssd_kernel(xbc_ref, z_ref, dt_ref, cw_ref, cb_ref, dtb_ref, a_ref, dsk_ref, nw_ref,
                y_ref, xext_ref, st_ref, yg_ref):
    t = CHUNK
    c = pl.program_id(1)

    @pl.when(c == 0)
    def _():
        xext_ref[0:SUBLANES, :] = jnp.zeros((SUBLANES, SSD_CONV_DIM), F32)
        st_ref[...] = jnp.zeros_like(st_ref)

    @pl.when(c != 0)
    def _():
        xext_ref[0:SUBLANES, :] = xext_ref[t:t + SUBLANES, :]

    xext_ref[SUBLANES:SUBLANES + t, :] = xbc_ref[...].astype(F32)
    acc = jnp.broadcast_to(cb_ref[...], (t, SSD_CONV_DIM))
    for k in range(SSD_CONV):
        off = SUBLANES - (SSD_CONV - 1) + k
        acc = acc + cw_ref[k:k + 1, :] * xext_ref[off:off + t, :]
    xc = _silu(acc)
    gn = SSD_GROUPS * SSD_STATE
    xs = xc[:, :SSD_WIDTH]
    bm = xc[:, SSD_WIDTH:SSD_WIDTH + gn]
    cm = xc[:, SSD_WIDTH + gn:]

    xr = dt_ref[...] + dtb_ref[...]
    dt = jnp.maximum(xr, 0.0) + jnp.log1p(jnp.exp(-jnp.abs(xr)))
    da = dt * a_ref[...]
    r_i = lax.broadcasted_iota(jnp.int32, (t, t), 0)
    c_i = lax.broadcasted_iota(jnp.int32, (t, t), 1)
    causal = r_i >= c_i
    tri = causal.astype(BF16)
    a_cs = sum(_dot(tri, part) for part in _split3(da))
    e_r = lax.broadcasted_iota(jnp.int32, (LANES, SSD_WIDTH), 0)
    e_c = lax.broadcasted_iota(jnp.int32, (LANES, SSD_WIDTH), 1)
    expand = (lax.shift_right_logical(e_c, 6) == e_r).astype(BF16)
    dt_e = sum(_dot(part, expand) for part in _split3(dt))
    acs_e = sum(_dot(part, expand) for part in _split3(a_cs))
    last_e = acs_e[t - 1:t, :]
    xdt = xs * dt_e
    xdt_b = xdt.astype(BF16)
    xd_b = (xdt * jnp.exp(last_e - acs_e)).astype(BF16)
    exp_acs = jnp.exp(acs_e)
    chunk_decay = jnp.exp(last_e)
    a_cs_t = a_cs.T
    lane = lax.broadcasted_iota(jnp.int32, (t, LANES), 1)
    gw = SSD_REP * SSD_HEAD_DIM
    for g in range(SSD_GROUPS):
        bg = bm[:, g * SSD_STATE:(g + 1) * SSD_STATE]
        cg = cm[:, g * SSD_STATE:(g + 1) * SSD_STATE].astype(BF16)
        gmat = _dot_nt(cg, bg.astype(BF16))
        s_prev = st_ref[g]
        y_off = _dot(cg, s_prev.astype(BF16)) * exp_acs[:, g * gw:(g + 1) * gw]
        st_ref[g] = (s_prev * chunk_decay[:, g * gw:(g + 1) * gw]
                     + _dot(bg.T.astype(BF16), xd_b[:, g * gw:(g + 1) * gw]))
        for pr in range(SSD_REP // 2):
            ms = []
            for hh in range(2):
                hd = g * SSD_REP + 2 * pr + hh
                seg = a_cs[:, hd:hd + 1] - a_cs_t[hd:hd + 1, :]
                dec = jnp.where(causal, jnp.exp(jnp.where(causal, seg, 0.0)), 0.0)
                ms.append((gmat * dec).astype(BF16))
            lo = g * gw + pr * LANES
            x2 = xdt_b[:, lo:lo + LANES]
            rhs = jnp.concatenate([jnp.where(lane < SSD_HEAD_DIM, x2, jnp.zeros_like(x2)),
                                   jnp.where(lane >= SSD_HEAD_DIM, x2, jnp.zeros_like(x2))], axis=0)
            y_d = _dot(jnp.concatenate(ms, axis=1), rhs)
            y = y_d + y_off[:, pr * LANES:(pr + 1) * LANES] + xs[:, lo:lo + LANES] * dsk_ref[:, lo:lo + LANES]
            yg_ref[:, lo:lo + LANES] = y * _silu(z_ref[:, lo:lo + LANES].astype(F32))
    for g in range(SSD_GROUPS):
        yg = yg_ref[:, g * gw:(g + 1) * gw]
        ms_ = jnp.mean(yg * yg, axis=-1, keepdims=True)
        y_ref[:, g * gw:(g + 1) * gw] = (yg * lax.rsqrt(ms_ + NORM_EPS)
                                         * nw_ref[:, g * gw:(g + 1) * gw]).astype(y_ref.dtype)


def _ssd(xbc, z, dt, conv_w, conv_b, dt_bias, a_neg, d_e, norm_w, batch, n_chunks):
    t = CHUNK
    n = xbc.shape[0]
    row = lambda b, c: (b * n_chunks + c, 0)
    const = lambda b, c: (0, 0)
    blk = (_nbytes((t, SSD_CONV_DIM), BF16) + 2 * _nbytes((t, SSD_WIDTH), BF16)
           + _nbytes((t, LANES), F32) + 6 * _nbytes((SUBLANES, SSD_CONV_DIM), F32))
    scratch = (_nbytes((t + SUBLANES, SSD_CONV_DIM), F32) + _nbytes((SSD_GROUPS, SSD_STATE, 512), F32)
               + _nbytes((t, SSD_WIDTH), F32))
    return pl.pallas_call(
        _ssd_kernel, grid=(batch, n_chunks),
        in_specs=[pl.BlockSpec((t, SSD_CONV_DIM), row),
                  pl.BlockSpec((t, SSD_WIDTH), lambda b, c: (b * n_chunks + c, 1)),
                  pl.BlockSpec((t, LANES), row),
                  pl.BlockSpec(conv_w.shape, const), pl.BlockSpec(conv_b.shape, const),
                  pl.BlockSpec(dt_bias.shape, const), pl.BlockSpec(a_neg.shape, const),
                  pl.BlockSpec(d_e.shape, const), pl.BlockSpec(norm_w.shape, const)],
        out_specs=pl.BlockSpec((t, SSD_WIDTH), row),
        out_shape=jax.ShapeDtypeStruct((n, SSD_WIDTH), BF16),
        scratch_shapes=[pltpu.VMEM((t + SUBLANES, SSD_CONV_DIM), F32),
                        pltpu.VMEM((SSD_GROUPS, SSD_STATE, SSD_REP * SSD_HEAD_DIM), F32),
                        pltpu.VMEM((t, SSD_WIDTH), F32)],
        compiler_params=_params(("arbitrary", "arbitrary"), blk, scratch),
        name="ssd")(xbc, z, dt, conv_w, conv_b, dt_bias, a_neg, d_e, norm_w)


def _post0_kernel(ys_ref, u_ref, z_ref, yb_ref, x_ref, d_ref, gw_ref, gb_ref, wa_ref, wb_ref, o_ref):
    u = u_ref[...].astype(F32)
    v = ys_ref[...].astype(F32) + d_ref[...] * u
    y = 0.5 * v * (1.0 + jnp.tanh(math.sqrt(2.0 / math.pi) * (v + 0.044715 * (v * v * v))))
    gl = _dot(y.astype(BF16), gw_ref[...]) + gb_ref[...]
    ya = y * jax.nn.sigmoid(gl) * _silu(z_ref[...].astype(F32))
    out = _dot(ya.astype(BF16), wa_ref[...]) + _dot(yb_ref[...].astype(BF16), wb_ref[...])
    o_ref[...] = x_ref[...] + out


def _post0(ys, u, z, yb, x2d, s5_d, glu_w, glu_b, w_a, w_b):
    n, d = x2d.shape
    tm = min(TOKEN_TILE, n)
    row = lambda i: (i, 0)
    const = lambda i: (0, 0)
    blk = (3 * _nbytes((tm, d), BF16) + 3 * _nbytes((tm, d), F32) + 3 * _nbytes((d, d), BF16)
           + 2 * _nbytes((SUBLANES, d), F32))
    return pl.pallas_call(
        _post0_kernel, grid=(n // tm,),
        in_specs=[pl.BlockSpec((tm, d), row), pl.BlockSpec((tm, d), row), pl.BlockSpec((tm, d), row),
                  pl.BlockSpec((tm, d), row), pl.BlockSpec((tm, d), row),
                  pl.BlockSpec((1, d), const), pl.BlockSpec(glu_w.shape, const), pl.BlockSpec((1, d), const),
                  pl.BlockSpec(w_a.shape, const), pl.BlockSpec(w_b.shape, const)],
        out_specs=pl.BlockSpec((tm, d), row),
        out_shape=jax.ShapeDtypeStruct((n, d), F32),
        compiler_params=_params(("parallel",), blk), name="post0")(
            ys, u, z, yb, x2d, s5_d, glu_w, glu_b, w_a, w_b)


def _ret_kernel(q_ref, k_ref, v_ref, g_ref, dmat_ref, qd_ref, kd_ref, cd_ref, gw_ref, gb_ref,
                y_ref, st_ref):
    c = pl.program_id(1)

    @pl.when(c == 0)
    def _():
        st_ref[...] = jnp.zeros_like(st_ref)

    for h in range(RET_HEADS):
        qh = q_ref[:, h * RET_QK_DIM:(h + 1) * RET_QK_DIM]
        kh = k_ref[:, h * RET_QK_DIM:(h + 1) * RET_QK_DIM]
        vh = v_ref[:, h * RET_V_DIM:(h + 1) * RET_V_DIM]
        scores = _dot_nt(qh, kh) * dmat_ref[h]
        inner = _dot(scores.astype(BF16), vh)
        s_prev = st_ref[h]
        qdec = (qh.astype(F32) * qd_ref[h]).astype(BF16)
        cross = _dot(qdec, s_prev.astype(BF16))
        kdec = kh.astype(F32) * kd_ref[h]
        st_ref[h] = s_prev * cd_ref[h] + _dot(kdec.T.astype(BF16), vh)
        o = inner + cross
        mu = jnp.mean(o, axis=-1, keepdims=True)
        oc = o - mu
        var = jnp.mean(oc * oc, axis=-1, keepdims=True)
        sl = slice(h * RET_V_DIM, (h + 1) * RET_V_DIM)
        on = oc * lax.rsqrt(var + NORM_EPS) * gw_ref[:, sl] + gb_ref[:, sl]
        y_ref[:, sl] = (_silu(g_ref[:, sl].astype(F32)) * on).astype(y_ref.dtype)


def _retention(q, k, v, g, dmat, qd, kd, cd, gn_w, gn_b, batch, seq_len):
    rt = min(RET_CHUNK, seq_len)
    n_chunks = seq_len // rt
    n = q.shape[0]
    row = lambda b, c: (b * n_chunks + c, 0)
    c2 = lambda b, c: (0, 0)
    c3 = lambda b, c: (0, 0, 0)
    blk = (2 * _nbytes((rt, D_MODEL), BF16) + 3 * _nbytes((rt, D_INNER), BF16)
           + _nbytes(dmat.shape, F32) + 2 * _nbytes(qd.shape, F32) + _nbytes(cd.shape, F32)
           + 2 * _nbytes((SUBLANES, D_INNER), F32))
    scratch = _nbytes((RET_HEADS, RET_QK_DIM, RET_V_DIM), F32)
    return pl.pallas_call(
        _ret_kernel, grid=(batch, n_chunks),
        in_specs=[pl.BlockSpec((rt, D_MODEL), row), pl.BlockSpec((rt, D_MODEL), row),
                  pl.BlockSpec((rt, D_INNER), row), pl.BlockSpec((rt, D_INNER), row),
                  pl.BlockSpec(dmat.shape, c3), pl.BlockSpec(qd.shape, c3), pl.BlockSpec(kd.shape, c3),
                  pl.BlockSpec(cd.shape, c3), pl.BlockSpec((1, D_INNER), c2), pl.BlockSpec((1, D_INNER), c2)],
        out_specs=pl.BlockSpec((rt, D_INNER), row),
        out_shape=jax.ShapeDtypeStruct((n, D_INNER), BF16),
        scratch_shapes=[pltpu.VMEM((RET_HEADS, RET_QK_DIM, RET_V_DIM), F32)],
        compiler_params=_params(("arbitrary", "arbitrary"), blk, scratch),
        name="retention")(q, k, v, g, dmat, qd, kd, cd, gn_w, gn_b)


def _out_proj_kernel(y_ref, h_ref, w_ref, fw_ref, o_ref, *, final_norm):
    hn = h_ref[...] + _dot(y_ref[...], w_ref[...])
    if final_norm:
        ms = jnp.mean(hn * hn, axis=-1, keepdims=True)
        hn = hn * lax.rsqrt(ms + NORM_EPS) * fw_ref[...]
    o_ref[...] = hn


def _out_proj(y, h2d, w, final_w, final_norm):
    n, d = h2d.shape
    tm = min(TOKEN_TILE, n)
    row = lambda i: (i, 0)
    const = lambda i: (0, 0)
    blk = (_nbytes((tm, y.shape[1]), BF16) + 2 * _nbytes((tm, d), F32) + _nbytes(w.shape, BF16)
           + _nbytes((SUBLANES, d), F32))
    return pl.pallas_call(
        functools.partial(_out_proj_kernel, final_norm=final_norm), grid=(n // tm,),
        in_specs=[pl.BlockSpec((tm, y.shape[1]), row), pl.BlockSpec((tm, d), row),
                  pl.BlockSpec(w.shape, const), pl.BlockSpec((1, d), const)],
        out_specs=pl.BlockSpec((tm, d), row),
        out_shape=jax.ShapeDtypeStruct((n, d), F32),
        compiler_params=_params(("parallel",), blk), name="out_proj")(y, h2d, w, final_w)


def _s5_ssd_layer(h2d, norm_w, batch, seq_len, w_in, lam_re, lam_im, log_dt, b_re, b_im, c_re, c_im,
                  s5_d, glu_w, glu_b, conv_w, conv_b, dt_bias, a_log, ssd_d, ssd_norm_w, w_out):
    t = CHUNK
    n_chunks = seq_len // t
    n = batch * seq_len
    o1 = D_INNER
    o2 = o1 + S5_WIDTH
    o3 = o2 + SSD_CONV_DIM
    w_z = w_in[:, :o1].astype(BF16)
    w_u = w_in[:, o1:o2].astype(BF16)
    w_x = w_in[:, o2:o3].astype(BF16)
    w_dt = jnp.pad(w_in[:, o3:], ((0, 0), (0, LANES - SSD_HEADS))).astype(BF16)
    z, u, xbc, dt = _norm_proj(h2d, norm_w, [w_z, w_u, w_x, w_dt], [BF16, BF16, BF16, F32])

    rcat, apcat, pmat, qmat, sc = _s5_prepare(lam_re, lam_im, log_dt, b_re, b_im, c_re, c_im, n_chunks)
    taps = _s5_taps(rcat, apcat)
    u2 = u.reshape(batch * n_chunks, t, S5_WIDTH).transpose(0, 2, 1).reshape(batch * n_chunks, S5_WIDTH * t)
    y2 = _s5_core(u2, taps, pmat, qmat, sc, n_chunks)
    ys = y2.reshape(batch * n_chunks, S5_WIDTH, t).transpose(0, 2, 1).reshape(n, S5_WIDTH)

    pad_h = LANES - SSD_HEADS
    cw = jnp.pad(conv_w.astype(F32), ((0, SUBLANES - SSD_CONV), (0, 0)))
    cb = conv_b.astype(F32).reshape(1, SSD_CONV_DIM)
    dtb = jnp.pad(dt_bias.astype(F32), (0, pad_h)).reshape(1, LANES)
    a_neg = jnp.pad(-jnp.exp(a_log.astype(F32)), (0, pad_h)).reshape(1, LANES)
    d_e = jnp.repeat(ssd_d.astype(F32), SSD_HEAD_DIM).reshape(1, SSD_WIDTH)
    nw = ssd_norm_w.astype(F32).reshape(1, SSD_WIDTH)
    yb = _ssd(xbc, z, dt, cw, cb, dtb, a_neg, d_e, nw, batch, n_chunks)

    return _post0(ys, u, z, yb, h2d, s5_d.astype(F32).reshape(1, S5_WIDTH), glu_w.astype(BF16),
                  glu_b.astype(F32).reshape(1, S5_WIDTH), w_out[:S5_WIDTH].astype(BF16),
                  w_out[S5_WIDTH:].astype(BF16))


def _retention_layer(h2d, norm_w, batch, seq_len, w_in, gn_w, gn_b, w_out, final_w, final_norm):
    qk_w = RET_HEADS * RET_QK_DIM
    half = RET_QK_DIM // 2

    def deinterleave(w):
        return w.reshape(D_MODEL, RET_HEADS, half, 2).transpose(0, 1, 3, 2).reshape(D_MODEL, qk_w)

    w_q = deinterleave(w_in[:, :qk_w]).astype(BF16)
    w_k = deinterleave(w_in[:, qk_w:2 * qk_w]).astype(BF16)
    w_v = w_in[:, 2 * qk_w:2 * qk_w + D_INNER].astype(BF16)
    w_g = w_in[:, 2 * qk_w + D_INNER:].astype(BF16)
    pos = jnp.arange(seq_len, dtype=F32)
    angle = 1.0 / (ROPE_BASE ** jnp.linspace(0.0, 1.0, half, dtype=F32))
    theta = pos[:, None] * angle[None, :]
    q, k, v, g = _norm_proj(h2d, norm_w, [w_q, w_k, w_v, w_g], [BF16] * 4,
                            rot=(jnp.cos(theta), jnp.sin(theta)), seq_len=seq_len)

    rt = min(RET_CHUNK, seq_len)
    log_gamma = jnp.log(1.0 - 2.0 ** (-5.0 - jnp.arange(RET_HEADS, dtype=F32)))
    p = jnp.arange(rt, dtype=F32)
    rel = p[:, None] - p[None, :]
    dmat = jnp.where(rel >= 0, jnp.exp(log_gamma[:, None, None] * jnp.maximum(rel, 0.0)), 0.0)
    qd = jnp.broadcast_to(jnp.exp(log_gamma[:, None] * (p[None, :] + 1.0))[..., None],
                          (RET_HEADS, rt, RET_QK_DIM))
    kd = jnp.broadcast_to(jnp.exp(log_gamma[:, None] * (rt - 1.0 - p)[None, :])[..., None],
                          (RET_HEADS, rt, RET_QK_DIM))
    cd = jnp.broadcast_to(jnp.exp(log_gamma * rt)[:, None, None], (RET_HEADS, 1, RET_V_DIM))
    y = _retention(q, k, v, g, dmat, qd, kd, cd, gn_w.astype(F32).reshape(1, D_INNER),
                   gn_b.astype(F32).reshape(1, D_INNER), batch, seq_len)
    return _out_proj(y, h2d, w_out.astype(BF16), final_w.astype(F32).reshape(1, D_MODEL), final_norm)


def kernel(x, layer_norm_w, ab_w_in, s5_lam_re, s5_lam_im, s5_log_dt, s5_b_re, s5_b_im, s5_c_re, s5_c_im,
           s5_d, s5_glu_w, s5_glu_b, ssd_conv_w, ssd_conv_b, ssd_dt_bias, ssd_a_log, ssd_d, ssd_norm_w,
           ab_w_out, ret_w_in, ret_gn_w, ret_gn_b, ret_w_out, final_norm_w):
    batch, seq_len, d = x.shape
    depth = layer_norm_w.shape[0]
    assert depth % 2 == 0 and seq_len % CHUNK == 0 and d == D_MODEL
    h = x.astype(F32).reshape(batch * seq_len, d)
    for i in range(depth):
        j = i // 2
        if i % 2 == 0:
            h = _s5_ssd_layer(h, layer_norm_w[i], batch, seq_len, ab_w_in[j], s5_lam_re[j], s5_lam_im[j],
                              s5_log_dt[j], s5_b_re[j], s5_b_im[j], s5_c_re[j], s5_c_im[j], s5_d[j],
                              s5_glu_w[j], s5_glu_b[j], ssd_conv_w[j], ssd_conv_b[j], ssd_dt_bias[j],
                              ssd_a_log[j], ssd_d[j], ssd_norm_w[j], ab_w_out[j])
        else:
            h = _retention_layer(h, layer_norm_w[i], batch, seq_len, ret_w_in[j], ret_gn_w[j], ret_gn_b[j],
                                 ret_w_out[j], final_norm_w, final_norm=(i == depth - 1))
    return h.reshape(batch, seq_len, d).astype(x.dtype)
```

```python
import functools
import math

import jax
import jax.numpy as jnp
from jax import lax
from jax.experimental import pallas as pl
from jax.experimental.pallas import tpu as pltpu

F32 = jnp.float32
BF16 = jnp.bfloat16

D_MODEL = 1024
D_INNER = 2048
CHUNK = 128
NORM_EPS = 1e-6
S5_WIDTH = 1024
S5_GROUP = 16
S5_GROUPS = 64
S5_STATE = 64
SSD_WIDTH = 1024
SSD_HEAD_DIM = 64
SSD_HEADS = 16
SSD_GROUPS = 2
SSD_REP = 8
SSD_STATE = 128
SSD_CONV = 4
SSD_CONV_DIM = 1536
RET_HEADS = 4
RET_QK_DIM = 256
RET_V_DIM = 512
ROPE_BASE = 10000.0

LANES = 128
SUBLANES = 8
MXU_DIM = 256
VMEM_CAP = 60 * 1024 * 1024
TOKEN_TILE = 512
S5_CHUNK = 64
RET_CHUNK = 256


def _params(semantics, block_bytes, scratch_bytes=0):
    need = 2 * block_bytes + scratch_bytes + 16 * 1024 * 1024
    return pltpu.CompilerParams(dimension_semantics=semantics,
                                vmem_limit_bytes=int(min(need, VMEM_CAP)))


def _nbytes(shape, dtype):
    return math.prod(shape) * jnp.dtype(dtype).itemsize


def _silu(x):
    return x * jax.nn.sigmoid(x)


def _split3(x):
    hi = x.astype(BF16)
    r1 = x - hi.astype(F32)
    mid = r1.astype(BF16)
    lo = (r1 - mid.astype(F32)).astype(BF16)
    return hi, mid, lo


def _dot(a, b):
    return jnp.dot(a, b, preferred_element_type=F32)


def _dot_nt(a, b):
    return lax.dot_general(a, b, (((1,), (1,)), ((), ())), preferred_element_type=F32)


def _rmsnorm_bf16(x, w):
    ms = jnp.mean(x * x, axis=-1, keepdims=True)
    return (x * lax.rsqrt(ms + NORM_EPS) * w).astype(BF16)


def _norm_proj_kernel(*refs, n_out):
    x_ref, nw_ref = refs[0], refs[1]
    w_refs = refs[2:2 + n_out]
    o_refs = refs[2 + n_out:2 + 2 * n_out]
    xn = _rmsnorm_bf16(x_ref[...], nw_ref[...])
    for w_ref, o_ref in zip(w_refs, o_refs):
        o_ref[...] = _dot(xn, w_ref[...]).astype(o_ref.dtype)


def _norm_proj(x2d, norm_w, weights, out_dtypes):
    n, d = x2d.shape
    tm = min(TOKEN_TILE, n)
    row = lambda i: (i, 0)
    const = lambda i: (0, 0)
    in_specs = [pl.BlockSpec((tm, d), row), pl.BlockSpec((1, d), const)]
    blk = _nbytes((tm, d), F32)
    out_specs, out_shape = [], []
    for w, dt in zip(weights, out_dtypes):
        in_specs.append(pl.BlockSpec(w.shape, const))
        out_specs.append(pl.BlockSpec((tm, w.shape[1]), row))
        out_shape.append(jax.ShapeDtypeStruct((n, w.shape[1]), dt))
        blk += _nbytes(w.shape, w.dtype) + _nbytes((tm, w.shape[1]), dt)
    return pl.pallas_call(
        functools.partial(_norm_proj_kernel, n_out=len(weights)), grid=(n // tm,),
        in_specs=in_specs, out_specs=out_specs, out_shape=out_shape,
        compiler_params=_params(("parallel",), blk), name="norm_proj")(
            x2d, norm_w.reshape(1, d), *weights)


def _ret_proj_kernel(x_ref, nw_ref, cos_ref, sin_ref, cost_ref, sint_ref, wq_ref, wkt_ref, wv_ref, wg_ref,
                     q_ref, kt_ref, v_ref, g_ref):
    xn = _rmsnorm_bf16(x_ref[...], nw_ref[...])
    half = RET_QK_DIM // 2
    accq = _dot(xn, wq_ref[...])
    cos = cos_ref[...]
    sin = sin_ref[...]
    for h in range(RET_HEADS):
        x1 = accq[:, h * RET_QK_DIM:h * RET_QK_DIM + half]
        x2 = accq[:, h * RET_QK_DIM + half:(h + 1) * RET_QK_DIM]
        q_ref[:, h * RET_QK_DIM:h * RET_QK_DIM + half] = (x1 * cos - x2 * sin).astype(q_ref.dtype)
        q_ref[:, h * RET_QK_DIM + half:(h + 1) * RET_QK_DIM] = (x2 * cos + x1 * sin).astype(q_ref.dtype)
    acck = _dot_nt(wkt_ref[...], xn)
    cost = cost_ref[...]
    sint = sint_ref[...]
    scale = RET_QK_DIM ** -0.5
    for h in range(RET_HEADS):
        x1 = acck[h * RET_QK_DIM:h * RET_QK_DIM + half, :]
        x2 = acck[h * RET_QK_DIM + half:(h + 1) * RET_QK_DIM, :]
        kt_ref[h * RET_QK_DIM:h * RET_QK_DIM + half, :] = ((x1 * cost - x2 * sint) * scale).astype(kt_ref.dtype)
        kt_ref[h * RET_QK_DIM + half:(h + 1) * RET_QK_DIM, :] = ((x2 * cost + x1 * sint) * scale).astype(kt_ref.dtype)
    v_ref[...] = _dot(xn, wv_ref[...]).astype(v_ref.dtype)
    g_ref[...] = _dot(xn, wg_ref[...]).astype(g_ref.dtype)


def _ret_proj(x2d, norm_w, cos, sin, w_q, w_kt, w_v, w_g, seq_len):
    n, d = x2d.shape
    tm = min(TOKEN_TILE, seq_len)
    tiles_per_seq = seq_len // tm
    half = cos.shape[1]
    qk_w = w_q.shape[1]
    row = lambda i: (i, 0)
    col = lambda i: (0, i)
    const = lambda i: (0, 0)
    blk = (_nbytes((tm, d), F32) + 4 * _nbytes((tm, half), F32) + 2 * _nbytes(w_q.shape, BF16)
           + 2 * _nbytes(w_v.shape, BF16) + 2 * _nbytes((tm, qk_w), BF16) + 2 * _nbytes((tm, D_INNER), BF16))
    return pl.pallas_call(
        _ret_proj_kernel, grid=(n // tm,),
        in_specs=[pl.BlockSpec((tm, d), row), pl.BlockSpec((1, d), const),
                  pl.BlockSpec((tm, half), lambda i: (i % tiles_per_seq, 0)),
                  pl.BlockSpec((tm, half), lambda i: (i % tiles_per_seq, 0)),
                  pl.BlockSpec((half, tm), lambda i: (0, i % tiles_per_seq)),
                  pl.BlockSpec((half, tm), lambda i: (0, i % tiles_per_seq)),
                  pl.BlockSpec(w_q.shape, const), pl.BlockSpec(w_kt.shape, const),
                  pl.BlockSpec(w_v.shape, const), pl.BlockSpec(w_g.shape, const)],
        out_specs=[pl.BlockSpec((tm, qk_w), row), pl.BlockSpec((qk_w, tm), col),
                   pl.BlockSpec((tm, D_INNER), row), pl.BlockSpec((tm, D_INNER), row)],
        out_shape=[jax.ShapeDtypeStruct((n, qk_w), BF16), jax.ShapeDtypeStruct((qk_w, n), BF16),
                   jax.ShapeDtypeStruct((n, D_INNER), BF16), jax.ShapeDtypeStruct((n, D_INNER), BF16)],
        compiler_params=_params(("parallel",), blk), name="ret_proj")(
            x2d, norm_w.reshape(1, d), cos, sin, cos.T, sin.T, w_q, w_kt, w_v, w_g)


def _s5_taps_kernel(r_ref, a_ref, k_ref):
    k_ref[...] = jnp.dot(r_ref[...], a_ref[...], preferred_element_type=F32,
                         precision=lax.Precision.HIGHEST)


def _s5_taps(rcat, apcat):
    g, m, kk = rcat.shape
    t = apcat.shape[2]
    return pl.pallas_call(
        _s5_taps_kernel, grid=(g,),
        in_specs=[pl.BlockSpec((None, m, kk), lambda i: (i, 0, 0)),
                  pl.BlockSpec((None, kk, t), lambda i: (i, 0, 0))],
        out_specs=pl.BlockSpec((None, m, t), lambda i: (i, 0, 0)),
        out_shape=jax.ShapeDtypeStruct((g, m, t), F32),
        compiler_params=_params(("parallel",), _nbytes((m, kk + t), F32) + _nbytes((kk, t), F32)),
        name="s5_taps")(rcat, apcat)


def _s5_core_kernel(u_ref, k_ref, f_ref, bt_ref, a_ref, ct_ref, sc_ref, y_ref, w_ref, *, n_chunks):
    ts = S5_CHUNK
    h = S5_GROUP
    half = S5_STATE
    rows = u_ref.shape[0]
    u = u_ref[...]

    f1 = f_ref[0]
    f2 = f_ref[1]
    pmat = jnp.concatenate(
        [(f1 * bt_ref[0, hi:hi + 1, :] + f2 * bt_ref[1, hi:hi + 1, :]).astype(BF16) for hi in range(h)], axis=0)
    e = _dot(u, pmat)

    row_idx = lax.broadcasted_iota(jnp.int32, (n_chunks, 2 * half), 0)
    hprev = []
    for b in range(rows // n_chunks):
        acc = e[b * n_chunks:(b + 1) * n_chunks, :]
        d = 1
        lvl = 0
        while d < n_chunks:
            sh = jnp.where(row_idx >= d, pltpu.roll(acc, d, 0), 0.0)
            acc = (acc + sc_ref[2 * lvl:2 * lvl + 1, :] * sh
                   + sc_ref[2 * lvl + 1:2 * lvl + 2, :] * pltpu.roll(sh, half, 1))
            d *= 2
            lvl += 1
        hprev.append(jnp.where(row_idx >= 1, pltpu.roll(acc, 1, 0), 0.0))
    hp = jnp.concatenate(hprev, axis=0).astype(BF16)

    s_idx = lax.broadcasted_iota(jnp.int32, (ts, LANES), 0)
    l_idx = lax.broadcasted_iota(jnp.int32, (ts, LANES), 1)
    keep = (l_idx & (ts - 1)) >= s_idx
    first = lax.broadcasted_iota(jnp.int32, (2 * half, LANES), 1) < ts
    slabs_per_tile = MXU_DIM // LANES
    kdim = h * ts
    for n in range(h * ts // MXU_DIM):
        for j in range(slabs_per_tile):
            m = n * slabs_per_tile + j
            cols = slice(j * LANES, (j + 1) * LANES)
            for hi in range(h):
                kb = jnp.broadcast_to(k_ref[hi * (h // 2) + m:hi * (h // 2) + m + 1, :], (ts, LANES))
                tz = pltpu.roll(kb, 0, 1, stride=1, stride_axis=0)
                w_ref[n, hi * ts:(hi + 1) * ts, cols] = jnp.where(keep, tz, 0.0).astype(BF16)
            k1 = jnp.where(first, jnp.broadcast_to(ct_ref[0, :, 2 * m:2 * m + 1], (2 * half, LANES)),
                           jnp.broadcast_to(ct_ref[0, :, 2 * m + 1:2 * m + 2], (2 * half, LANES)))
            k2 = jnp.where(first, jnp.broadcast_to(ct_ref[1, :, 2 * m:2 * m + 1], (2 * half, LANES)),
                           jnp.broadcast_to(ct_ref[1, :, 2 * m + 1:2 * m + 2], (2 * half, LANES)))
            w_ref[n, kdim:kdim + 2 * half, cols] = (k1 * a_ref[0] + k2 * a_ref[1]).astype(BF16)
        y = _dot(u, w_ref[n, 0:kdim, :]) + _dot(hp, w_ref[n, kdim:kdim + 2 * half, :])
        y_ref[:, n * MXU_DIM:(n + 1) * MXU_DIM] = y.astype(y_ref.dtype)


def _s5_core(u2, taps2, ftab, bttab, atab, cttab, sc, n_chunks):
    rows = u2.shape[0]
    g = taps2.shape[0]
    w = S5_GROUP * S5_CHUNK
    tabs = [taps2, ftab, bttab, atab, cttab, sc]
    blk = 2 * _nbytes((rows, w), BF16) + sum(_nbytes(t.shape[1:], F32) for t in tabs)
    wshape = (w // MXU_DIM, w + 2 * S5_STATE, MXU_DIM)

    def tab_spec(t):
        nd = t.ndim - 1
        return pl.BlockSpec((None,) + t.shape[1:], lambda i: (i,) + (0,) * nd)

    return pl.pallas_call(
        functools.partial(_s5_core_kernel, n_chunks=n_chunks), grid=(g,),
        in_specs=[pl.BlockSpec((rows, w), lambda i: (0, i))] + [tab_spec(t) for t in tabs],
        out_specs=pl.BlockSpec((rows, w), lambda i: (0, i)),
        out_shape=jax.ShapeDtypeStruct((rows, g * w), BF16),
        scratch_shapes=[pltpu.VMEM(wshape, BF16)],
        compiler_params=_params(("parallel",), blk, _nbytes(wshape, BF16)),
        name="s5_core")(u2, *tabs)


def _s5_prepare(lam_re, lam_im, log_dt, b_re, b_im, c_re, c_im, n_chunks):
    ts = S5_CHUNK
    assert 2 * ts == LANES and 2 * S5_STATE == LANES
    dt = jnp.exp(log_dt.astype(F32))[:, None]
    lr = jnp.minimum(lam_re.astype(F32), -1e-4)
    li = lam_im.astype(F32)
    mag = jnp.exp(lr * dt)
    ab_re = mag * jnp.cos(li * dt)
    ab_im = mag * jnp.sin(li * dt)
    den = lr * lr + li * li
    nr = ab_re - 1.0
    coef_re = (nr * lr + ab_im * li) / den
    coef_im = (ab_im * lr - nr * li) / den
    br = b_re.astype(F32)
    bi = b_im.astype(F32)
    bb_re = coef_re[..., None] * br - coef_im[..., None] * bi
    bb_im = coef_re[..., None] * bi + coef_im[..., None] * br
    cr = c_re.astype(F32)
    ci = c_im.astype(F32)
    g = lr.shape[0]
    j = jnp.arange(ts + 1, dtype=F32)
    pmag = jnp.exp((lr * dt)[..., None] * j)
    ang = (li * dt)[..., None] * j
    pw_re = pmag * jnp.cos(ang)
    pw_im = pmag * jnp.sin(ang)
    bbt_re = bb_re.transpose(0, 2, 1)
    bbt_im = bb_im.transpose(0, 2, 1)
    r_re = cr[:, None] * bbt_re[:, :, None] - ci[:, None] * bbt_im[:, :, None]
    r_im = cr[:, None] * bbt_im[:, :, None] + ci[:, None] * bbt_re[:, :, None]
    rcat = jnp.concatenate([r_re, -r_im], axis=-1).reshape(g, S5_GROUP * S5_GROUP, 2 * S5_STATE)
    apcat = jnp.concatenate([pw_re[..., :ts], pw_im[..., :ts]], axis=1)
    fr = pw_re[..., ts - 1::-1].transpose(0, 2, 1)
    fi = pw_im[..., ts - 1::-1].transpose(0, 2, 1)
    ftab = jnp.stack([jnp.concatenate([fr, fr], -1), jnp.concatenate([-fi, fi], -1)], axis=1)
    bttab = jnp.stack([jnp.concatenate([bbt_re, bbt_im], -1), jnp.concatenate([bbt_im, bbt_re], -1)], axis=1)
    wr = pw_re[..., 1:]
    wi = pw_im[..., 1:]
    a1 = jnp.concatenate([wr, -wi], axis=1)
    a2 = jnp.concatenate([-wi, -wr], axis=1)
    atab = jnp.stack([jnp.concatenate([a1, a1], -1), jnp.concatenate([a2, a2], -1)], axis=1)
    crt = cr.transpose(0, 2, 1)
    cit = ci.transpose(0, 2, 1)
    cttab = jnp.stack([jnp.concatenate([crt, crt], 1), jnp.concatenate([cit, cit], 1)], axis=1)
    rows = []
    d = 1
    while d < n_chunks:
        ex = float(ts * d)
        m = jnp.exp(lr * dt * ex)
        a_r = m * jnp.cos(li * dt * ex)
        a_i = m * jnp.sin(li * dt * ex)
        rows.append(jnp.concatenate([a_r, a_r], axis=-1))
        rows.append(jnp.concatenate([-a_i, a_i], axis=-1))
        d *= 2
    while len(rows) % SUBLANES or not rows:
        rows.append(jnp.zeros((g, 2 * S5_STATE), F32))
    sc = jnp.stack(rows, axis=1)
    return rcat, apcat, ftab, bttab, atab, cttab, sc


def _ssd_kernel(xbc_ref, z_ref, dt_ref, shift_ref, tri_ref, exp_ref, cw_ref, cb_ref, dtb_ref, a_ref,
                dsk_ref, nw_ref, y_ref, xcat_ref, st_ref, yg_ref):
    t = CHUNK
    c = pl.program_id(1)

    @pl.when(c == 0)
    def _():
        xcat_ref[0:t, :] = jnp.zeros((t, SSD_CONV_DIM), BF16)
        st_ref[...] = jnp.zeros_like(st_ref)

    @pl.when(c != 0)
    def _():
        xcat_ref[0:t, :] = xcat_ref[t:2 * t, :]

    xcat_ref[t:2 * t, :] = xbc_ref[...]
    xsh = _dot(shift_ref[...], xcat_ref[...])
    acc = cb_ref[...] + cw_ref[SSD_CONV - 1:SSD_CONV, :] * xbc_ref[...].astype(F32)
    for k in range(SSD_CONV - 1):
        acc = acc + cw_ref[k:k + 1, :] * xsh[k * t:(k + 1) * t, :]
    xc = _silu(acc)
    gn = SSD_GROUPS * SSD_STATE
    xs = xc[:, :SSD_WIDTH]
    bm = xc[:, SSD_WIDTH:SSD_WIDTH + gn]
    cm = xc[:, SSD_WIDTH + gn:]

    xr = dt_ref[...] + dtb_ref[...]
    dt = jnp.maximum(xr, 0.0) + jnp.log(1.0 + jnp.exp(-jnp.abs(xr)))
    da = dt * a_ref[...]
    tri = tri_ref[...]
    a_cs = sum(_dot(tri, part) for part in _split3(da))
    expand = exp_ref[...]
    dt_e = sum(_dot(part, expand) for part in _split3(dt))
    acs_e = sum(_dot(part, expand) for part in _split3(a_cs))
    last_e = acs_e[t - 1:t, :]
    xdt = xs * dt_e
    xdt_b = xdt.astype(BF16)
    xd_b = (xdt * jnp.exp(last_e - acs_e)).astype(BF16)
    exp_acs = jnp.exp(acs_e)
    chunk_decay = jnp.exp(last_e)
    a_cs_t = a_cs.T
    r_i = lax.broadcasted_iota(jnp.int32, (t, t), 0)
    c_i = lax.broadcasted_iota(jnp.int32, (t, t), 1)
    causal = r_i >= c_i
    lane = lax.broadcasted_iota(jnp.int32, (t, LANES), 1)
    gw = SSD_REP * SSD_HEAD_DIM
    for g in range(SSD_GROUPS):
        bg = bm[:, g * SSD_STATE:(g + 1) * SSD_STATE]
        cg = cm[:, g * SSD_STATE:(g + 1) * SSD_STATE].astype(BF16)
        gmat = _dot_nt(cg, bg.astype(BF16))
        s_prev = st_ref[g]
        y_off = _dot(cg, s_prev.astype(BF16)) * exp_acs[:, g * gw:(g + 1) * gw]
        st_ref[g] = (s_prev * chunk_decay[:, g * gw:(g + 1) * gw]
                     + _dot(bg.T.astype(BF16), xd_b[:, g * gw:(g + 1) * gw]))
        for pr in range(SSD_REP // 2):
            ms = []
            for hh in range(2):
                hd = g * SSD_REP + 2 * pr + hh
                seg = a_cs[:, hd:hd + 1] - a_cs_t[hd:hd + 1, :]
                ms.append((gmat * jnp.exp(jnp.where(causal, seg, -jnp.inf))).astype(BF16))
            lo = g * gw + pr * LANES
            x2 = xdt_b[:, lo:lo + LANES]
            rhs = jnp.concatenate([jnp.where(lane < SSD_HEAD_DIM, x2, jnp.zeros_like(x2)),
                                   jnp.where(lane >= SSD_HEAD_DIM, x2, jnp.zeros_like(x2))], axis=0)
            y_d = _dot(jnp.concatenate(ms, axis=1), rhs)
            y = y_d + y_off[:, pr * LANES:(pr + 1) * LANES] + xs[:, lo:lo + LANES] * dsk_ref[:, lo:lo + LANES]
            yg_ref[:, lo:lo + LANES] = y * _silu(z_ref[:, lo:lo + LANES].astype(F32))
    for g in range(SSD_GROUPS):
        yg = yg_ref[:, g * gw:(g + 1) * gw]
        ms_ = jnp.mean(yg * yg, axis=-1, keepdims=True)
        y_ref[:, g * gw:(g + 1) * gw] = (yg * lax.rsqrt(ms_ + NORM_EPS)
                                         * nw_ref[:, g * gw:(g + 1) * gw]).astype(y_ref.dtype)


def _ssd(xbc, z, dt, conv_w, conv_b, dt_bias, a_neg, d_e, norm_w, batch, n_chunks):
    t = CHUNK
    n = xbc.shape[0]
    row = lambda b, c: (b * n_chunks + c, 0)
    const = lambda b, c: (0, 0)
    rr = jnp.arange((SSD_CONV - 1) * t)
    src = t + rr % t - (SSD_CONV - 1) + rr // t
    shift = (jnp.arange(2 * t)[None, :] == src[:, None]).astype(BF16)
    tri = (jnp.arange(t)[:, None] >= jnp.arange(t)[None, :]).astype(BF16)
    expand = (jnp.arange(SSD_WIDTH)[None, :] // SSD_HEAD_DIM == jnp.arange(LANES)[:, None]).astype(BF16)
    blk = (_nbytes((t, SSD_CONV_DIM), BF16) + 2 * _nbytes((t, SSD_WIDTH), BF16) + _nbytes((t, LANES), F32)
           + _nbytes(shift.shape, BF16) + _nbytes(tri.shape, BF16) + _nbytes(expand.shape, BF16)
           + 6 * _nbytes((SUBLANES, SSD_CONV_DIM), F32))
    scratch = (_nbytes((2 * t, SSD_CONV_DIM), BF16) + _nbytes((SSD_GROUPS, SSD_STATE, 512), F32)
               + _nbytes((t, SSD_WIDTH), F32))
    return pl.pallas_call(
        _ssd_kernel, grid=(batch, n_chunks),
        in_specs=[pl.BlockSpec((t, SSD_CONV_DIM), row),
                  pl.BlockSpec((t, SSD_WIDTH), lambda b, c: (b * n_chunks + c, 1)),
                  pl.BlockSpec((t, LANES), row),
                  pl.BlockSpec(shift.shape, const), pl.BlockSpec(tri.shape, const),
                  pl.BlockSpec(expand.shape, const),
                  pl.BlockSpec(conv_w.shape, const), pl.BlockSpec(conv_b.shape, const),
                  pl.BlockSpec(dt_bias.shape, const), pl.BlockSpec(a_neg.shape, const),
                  pl.BlockSpec(d_e.shape, const), pl.BlockSpec(norm_w.shape, const)],
        out_specs=pl.BlockSpec((t, SSD_WIDTH), row),
        out_shape=jax.ShapeDtypeStruct((n, SSD_WIDTH), BF16),
        scratch_shapes=[pltpu.VMEM((2 * t, SSD_CONV_DIM), BF16),
                        pltpu.VMEM((SSD_GROUPS, SSD_STATE, SSD_REP * SSD_HEAD_DIM), F32),
                        pltpu.VMEM((t, SSD_WIDTH), F32)],
        compiler_params=_params(("arbitrary", "arbitrary"), blk, scratch),
        name="ssd")(xbc, z, dt, shift, tri, expand, conv_w, conv_b, dt_bias, a_neg, d_e, norm_w)


def _post0_kernel(ys_ref, u_ref, z_ref, yb_ref, x_ref, d_ref, gw_ref, gb_ref, wa_ref, wb_ref, o_ref):
    u = u_ref[...].astype(F32)
    v = ys_ref[...].astype(F32) + d_ref[...] * u
    y = 0.5 * v * (1.0 + jnp.tanh(math.sqrt(2.0 / math.pi) * (v + 0.044715 * (v * v * v))))
    gl = _dot(y.astype(BF16), gw_ref[...]) + gb_ref[...]
    ya = y * jax.nn.sigmoid(gl) * _silu(z_ref[...].astype(F32))
    out = _dot(ya.astype(BF16), wa_ref[...]) + _dot(yb_ref[...].astype(BF16), wb_ref[...])
    o_ref[...] = x_ref[...] + out


def _post0(ys, u, z, yb, x2d, s5_d, glu_w, glu_b, w_a, w_b):
    n, d = x2d.shape
    tm = min(TOKEN_TILE, n)
    row = lambda i: (i, 0)
    const = lambda i: (0, 0)
    blk = (3 * _nbytes((tm, d), BF16) + 3 * _nbytes((tm, d), F32) + 3 * _nbytes((d, d), BF16)
           + 2 * _nbytes((SUBLANES, d), F32))
    return pl.pallas_call(
        _post0_kernel, grid=(n // tm,),
        in_specs=[pl.BlockSpec((tm, d), row), pl.BlockSpec((tm, d), row), pl.BlockSpec((tm, d), row),
                  pl.BlockSpec((tm, d), row), pl.BlockSpec((tm, d), row),
                  pl.BlockSpec((1, d), const), pl.BlockSpec(glu_w.shape, const), pl.BlockSpec((1, d), const),
                  pl.BlockSpec(w_a.shape, const), pl.BlockSpec(w_b.shape, const)],
        out_specs=pl.BlockSpec((tm, d), row),
        out_shape=jax.ShapeDtypeStruct((n, d), F32),
        compiler_params=_params(("parallel",), blk), name="post0")(
            ys, u, z, yb, x2d, s5_d, glu_w, glu_b, w_a, w_b)


def _ret_kernel(q_ref, kt_ref, v_ref, g_ref, dmat_ref, qd_ref, kdt_ref, cd_ref, gw_ref, gb_ref,
                y_ref, st_ref):
    c = pl.program_id(1)

    @pl.when(c == 0)
    def _():
        st_ref[...] = jnp.zeros_like(st_ref)

    for h in range(RET_HEADS):
        qh = q_ref[:, h * RET_QK_DIM:(h + 1) * RET_QK_DIM]
        kth = kt_ref[h * RET_QK_DIM:(h + 1) * RET_QK_DIM, :]
        vh = v_ref[:, h * RET_V_DIM:(h + 1) * RET_V_DIM]
        scores = _dot(qh, kth) * dmat_ref[h]
        inner = _dot(scores.astype(BF16), vh)
        s_prev = st_ref[h]
        qdec = (qh.astype(F32) * qd_ref[h]).astype(BF16)
        cross = _dot(qdec, s_prev.astype(BF16))
        kdec = (kth.astype(F32) * kdt_ref[h]).astype(BF16)
        st_ref[h] = s_prev * cd_ref[h] + _dot(kdec, vh)
        o = inner + cross
        mu = jnp.mean(o, axis=-1, keepdims=True)
        oc = o - mu
        var = jnp.mean(oc * oc, axis=-1, keepdims=True)
        sl = slice(h * RET_V_DIM, (h + 1) * RET_V_DIM)
        on = oc * lax.rsqrt(var + NORM_EPS) * gw_ref[:, sl] + gb_ref[:, sl]
        y_ref[:, sl] = (_silu(g_ref[:, sl].astype(F32)) * on).astype(y_ref.dtype)


def _retention(q, kt, v, g, dmat, qd, kdt, cd, gn_w, gn_b, batch, seq_len):
    rt = dmat.shape[1]
    n_chunks = seq_len // rt
    n = q.shape[0]
    row = lambda b, c: (b * n_chunks + c, 0)
    col = lambda b, c: (0, b * n_chunks + c)
    c2 = lambda b, c: (0, 0)
    c3 = lambda b, c: (0, 0, 0)
    blk = (2 * _nbytes((rt, D_MODEL), BF16) + 3 * _nbytes((rt, D_INNER), BF16)
           + _nbytes(dmat.shape, F32) + 2 * _nbytes(qd.shape, F32) + _nbytes(cd.shape, F32)
           + 2 * _nbytes((SUBLANES, D_INNER), F32))
    scratch = _nbytes((RET_HEADS, RET_QK_DIM, RET_V_DIM), F32)
    return pl.pallas_call(
        _ret_kernel, grid=(batch, n_chunks),
        in_specs=[pl.BlockSpec((rt, D_MODEL), row), pl.BlockSpec((D_MODEL, rt), col),
                  pl.BlockSpec((rt, D_INNER), row), pl.BlockSpec((rt, D_INNER), row),
                  pl.BlockSpec(dmat.shape, c3), pl.BlockSpec(qd.shape, c3), pl.BlockSpec(kdt.shape, c3),
                  pl.BlockSpec(cd.shape, c3), pl.BlockSpec((1, D_INNER), c2), pl.BlockSpec((1, D_INNER), c2)],
        out_specs=pl.BlockSpec((rt, D_INNER), row),
        out_shape=jax.ShapeDtypeStruct((n, D_INNER), BF16),
        scratch_shapes=[pltpu.VMEM((RET_HEADS, RET_QK_DIM, RET_V_DIM), F32)],
        compiler_params=_params(("arbitrary", "arbitrary"), blk, scratch),
        name="retention")(q, kt, v, g, dmat, qd, kdt, cd, gn_w, gn_b)


def _out_proj_kernel(y_ref, h_ref, w_ref, fw_ref, o_ref, *, final_norm):
    hn = h_ref[...] + _dot(y_ref[...], w_ref[...])
    if final_norm:
        ms = jnp.mean(hn * hn, axis=-1, keepdims=True)
        hn = hn * lax.rsqrt(ms + NORM_EPS) * fw_ref[...]
    o_ref[...] = hn


def _out_proj(y, h2d, w, final_w, final_norm):
    n, d = h2d.shape
    tm = min(TOKEN_TILE, n)
    row = lambda i: (i, 0)
    const = lambda i: (0, 0)
    blk = (_nbytes((tm, y.shape[1]), BF16) + 2 * _nbytes((tm, d), F32) + _nbytes(w.shape, BF16)
           + _nbytes((SUBLANES, d), F32))
    return pl.pallas_call(
        functools.partial(_out_proj_kernel, final_norm=final_norm), grid=(n // tm,),
        in_specs=[pl.BlockSpec((tm, y.shape[1]), row), pl.BlockSpec((tm, d), row),
                  pl.BlockSpec(w.shape, const), pl.BlockSpec((1, d), const)],
        out_specs=pl.BlockSpec((tm, d), row),
        out_shape=jax.ShapeDtypeStruct((n, d), F32),
        compiler_params=_params(("parallel",), blk), name="out_proj")(y, h2d, w, final_w)


def _s5_ssd_layer(h2d, norm_w, batch, seq_len, w_in, lam_re, lam_im, log_dt, b_re, b_im, c_re, c_im,
                  s5_d, glu_w, glu_b, conv_w, conv_b, dt_bias, a_log, ssd_d, ssd_norm_w, w_out):
    n = batch * seq_len
    o1 = D_INNER
    o2 = o1 + S5_WIDTH
    o3 = o2 + SSD_CONV_DIM
    w_z = w_in[:, :o1].astype(BF16)
    w_u = w_in[:, o1:o2].astype(BF16)
    w_x = w_in[:, o2:o3].astype(BF16)
    w_dt = jnp.pad(w_in[:, o3:], ((0, 0), (0, LANES - SSD_HEADS))).astype(BF16)
    z, u, xbc, dt = _norm_proj(h2d, norm_w, [w_z, w_u, w_x, w_dt], [BF16, BF16, BF16, F32])

    ts = S5_CHUNK
    nc_s = seq_len // ts
    rcat, apcat, ftab, bttab, atab, cttab, sc = _s5_prepare(lam_re, lam_im, log_dt, b_re, b_im, c_re, c_im, nc_s)
    taps2 = _s5_taps(rcat, apcat).reshape(S5_GROUPS, S5_GROUP * S5_GROUP // 2, 2 * ts)
    u2 = u.reshape(batch * nc_s, ts, S5_WIDTH).transpose(0, 2, 1).reshape(batch * nc_s, S5_WIDTH * ts)
    y2 = _s5_core(u2, taps2, ftab, bttab, atab, cttab, sc, nc_s)
    ys = y2.reshape(batch * nc_s, S5_WIDTH, ts).transpose(0, 2, 1).reshape(n, S5_WIDTH)

    pad_h = LANES - SSD_HEADS
    cw = jnp.pad(conv_w.astype(F32), ((0, SUBLANES - SSD_CONV), (0, 0)))
    cb = conv_b.astype(F32).reshape(1, SSD_CONV_DIM)
    dtb = jnp.pad(dt_bias.astype(F32), (0, pad_h)).reshape(1, LANES)
    a_neg = jnp.pad(-jnp.exp(a_log.astype(F32)), (0, pad_h)).reshape(1, LANES)
    d_e = jnp.repeat(ssd_d.astype(F32), SSD_HEAD_DIM).reshape(1, SSD_WIDTH)
    nw = ssd_norm_w.astype(F32).reshape(1, SSD_WIDTH)
    yb = _ssd(xbc, z, dt, cw, cb, dtb, a_neg, d_e, nw, batch, seq_len // CHUNK)

    return _post0(ys, u, z, yb, h2d, s5_d.astype(F32).reshape(1, S5_WIDTH), glu_w.astype(BF16),
                  glu_b.astype(F32).reshape(1, S5_WIDTH), w_out[:S5_WIDTH].astype(BF16),
                  w_out[S5_WIDTH:].astype(BF16))


def _retention_layer(h2d, norm_w, batch, seq_len, w_in, gn_w, gn_b, w_out, final_w, final_norm):
    qk_w = RET_HEADS * RET_QK_DIM
    half = RET_QK_DIM // 2

    def deinterleave(w):
        return w.reshape(D_MODEL, RET_HEADS, half, 2).transpose(0, 1, 3, 2).reshape(D_MODEL, qk_w)

    w_q = deinterleave(w_in[:, :qk_w]).astype(BF16)
    w_kt = deinterleave(w_in[:, qk_w:2 * qk_w]).T.astype(BF16)
    w_v = w_in[:, 2 * qk_w:2 * qk_w + D_INNER].astype(BF16)
    w_g = w_in[:, 2 * qk_w + D_INNER:].astype(BF16)
    pos = jnp.arange(seq_len, dtype=F32)
    angle = 1.0 / (ROPE_BASE ** jnp.linspace(0.0, 1.0, half, dtype=F32))
    theta = pos[:, None] * angle[None, :]
    q, kt, v, g = _ret_proj(h2d, norm_w, jnp.cos(theta), jnp.sin(theta), w_q, w_kt, w_v, w_g, seq_len)

    rt = min(RET_CHUNK, seq_len)
    log_gamma = jnp.log(1.0 - 2.0 ** (-5.0 - jnp.arange(RET_HEADS, dtype=F32)))
    p = jnp.arange(rt, dtype=F32)
    rel = p[:, None] - p[None, :]
    dmat = jnp.where(rel >= 0, jnp.exp(log_gamma[:, None, None] * jnp.maximum(rel, 0.0)), 0.0)
    qd = jnp.broadcast_to(jnp.exp(log_gamma[:, None] * (p[None, :] + 1.0))[..., None],
                          (RET_HEADS, rt, RET_QK_DIM))
    kdt = jnp.broadcast_to(jnp.exp(log_gamma[:, None] * (rt - 1.0 - p)[None, :])[:, None, :],
                           (RET_HEADS, RET_QK_DIM, rt))
    cd = jnp.broadcast_to(jnp.exp(log_gamma * rt)[:, None, None], (RET_HEADS, 1, RET_V_DIM))
    y = _retention(q, kt, v, g, dmat, qd, kdt, cd, gn_w.astype(F32).reshape(1, D_INNER),
                   gn_b.astype(F32).reshape(1, D_INNER), batch, seq_len)
    return _out_proj(y, h2d, w_out.astype(BF16), final_w.astype(F32).reshape(1, D_MODEL), final_norm)


def kernel(x, layer_norm_w, ab_w_in, s5_lam_re, s5_lam_im, s5_log_dt, s5_b_re, s5_b_im, s5_c_re, s5_c_im,
           s5_d, s5_glu_w, s5_glu_b, ssd_conv_w, ssd_conv_b, ssd_dt_bias, ssd_a_log, ssd_d, ssd_norm_w,
           ab_w_out, ret_w_in, ret_gn_w, ret_gn_b, ret_w_out, final_norm_w):
    batch, seq_len, d = x.shape
    depth = layer_norm_w.shape[0]
    assert depth % 2 == 0 and seq_len % CHUNK == 0 and d == D_MODEL
    h = x.astype(F32).reshape(batch * seq_len, d)
    for i in range(depth):
        j = i // 2
        if i % 2 == 0:
            h = _s5_ssd_layer(h, layer_norm_w[i], batch, seq_len, ab_w_in[j], s5_lam_re[j], s5_lam_im[j],
                              s5_log_dt[j], s5_b_re[j], s5_b_im[j], s5_c_re[j], s5_c_im[j], s5_d[j],
                              s5_glu_w[j], s5_glu_b[j], ssd_conv_w[j], ssd_conv_b[j], ssd_dt_bias[j],
                              ssd_a_log[j], ssd_d[j], ssd_norm_w[j], ab_w_out[j])
        else:
            h = _retention_layer(h, layer_norm_w[i], batch, seq_len, ret_w_in[j], ret_gn_w[j], ret_gn_b[j],
                                 ret_w_out[j], final_norm_w, final_norm=(i == depth - 1))
    return h.reshape(batch, seq_len, d).astype(x.dtype)
```

```python
import functools
import math

import jax
import jax.numpy as jnp
from jax import lax
from jax.experimental import pallas as pl
from jax.experimental.pallas import tpu as pltpu

F32 = jnp.float32
BF16 = jnp.bfloat16

D_MODEL = 1024
D_INNER = 2048
CHUNK = 128
NORM_EPS = 1e-6
S5_WIDTH = 1024
S5_GROUP = 16
S5_GROUPS = 64
S5_STATE = 64
SSD_WIDTH = 1024
SSD_HEAD_DIM = 64
SSD_HEADS = 16
SSD_GROUPS = 2
SSD_REP = 8
SSD_STATE = 128
SSD_CONV = 4
SSD_CONV_DIM = 1536
RET_HEADS = 4
RET_QK_DIM = 256
RET_V_DIM = 512
ROPE_BASE = 10000.0

LANES = 128
SUBLANES = 8
MXU_DIM = 256
VMEM_CAP = 60 * 1024 * 1024
TOKEN_TILE = 512
S5_CHUNK = 64
RET_CHUNK = 256


def _params(semantics, block_bytes, scratch_bytes=0):
    need = 2 * block_bytes + scratch_bytes + 16 * 1024 * 1024
    return pltpu.CompilerParams(dimension_semantics=semantics,
                                vmem_limit_bytes=int(min(need, VMEM_CAP)))


def _nbytes(shape, dtype):
    return math.prod(shape) * jnp.dtype(dtype).itemsize


def _silu(x):
    return x * jax.nn.sigmoid(x)


def _split3(x):
    hi = x.astype(BF16)
    r1 = x - hi.astype(F32)
    mid = r1.astype(BF16)
    lo = (r1 - mid.astype(F32)).astype(BF16)
    return hi, mid, lo


def _dot(a, b):
    return jnp.dot(a, b, preferred_element_type=F32)


def _dot_nt(a, b):
    return lax.dot_general(a, b, (((1,), (1,)), ((), ())), preferred_element_type=F32)


def _rmsnorm_bf16(x, w):
    ms = jnp.mean(x * x, axis=-1, keepdims=True)
    return (x * lax.rsqrt(ms + NORM_EPS) * w).astype(BF16)


def _norm_proj_kernel(*refs, n_out):
    x_ref, nw_ref = refs[0], refs[1]
    w_refs = refs[2:2 + n_out]
    o_refs = refs[2 + n_out:2 + 2 * n_out]
    xn = _rmsnorm_bf16(x_ref[...], nw_ref[...])
    for w_ref, o_ref in zip(w_refs, o_refs):
        o_ref[...] = _dot(xn, w_ref[...]).astype(o_ref.dtype)


def _norm_proj(x2d, norm_w, weights, out_dtypes):
    n, d = x2d.shape
    tm = min(TOKEN_TILE, n)
    row = lambda i: (i, 0)
    const = lambda i: (0, 0)
    in_specs = [pl.BlockSpec((tm, d), row), pl.BlockSpec((1, d), const)]
    blk = _nbytes((tm, d), F32)
    out_specs, out_shape = [], []
    for w, dt in zip(weights, out_dtypes):
        in_specs.append(pl.BlockSpec(w.shape, const))
        out_specs.append(pl.BlockSpec((tm, w.shape[1]), row))
        out_shape.append(jax.ShapeDtypeStruct((n, w.shape[1]), dt))
        blk += _nbytes(w.shape, w.dtype) + _nbytes((tm, w.shape[1]), dt)
    return pl.pallas_call(
        functools.partial(_norm_proj_kernel, n_out=len(weights)), grid=(n // tm,),
        in_specs=in_specs, out_specs=out_specs, out_shape=out_shape,
        compiler_params=_params(("parallel",), blk), name="norm_proj")(
            x2d, norm_w.reshape(1, d), *weights)


def _ret_proj_kernel(x_ref, nw_ref, cr_ref, sr_ref, cb_ref, sb_ref, crt_ref, srt_ref, cbt_ref, sbt_ref,
                     wq_ref, wkt_ref, wv_ref, wg_ref, q_ref, kt_ref, v_ref, g_ref):
    xn = _rmsnorm_bf16(x_ref[...], nw_ref[...])
    half = RET_QK_DIM // 2
    accq = _dot(xn, wq_ref[...])
    cos = cb_ref[...] * cr_ref[...] - sb_ref[...] * sr_ref[...]
    sin = sb_ref[...] * cr_ref[...] + cb_ref[...] * sr_ref[...]
    for h in range(RET_HEADS):
        x1 = accq[:, h * RET_QK_DIM:h * RET_QK_DIM + half]
        x2 = accq[:, h * RET_QK_DIM + half:(h + 1) * RET_QK_DIM]
        q_ref[:, h * RET_QK_DIM:h * RET_QK_DIM + half] = (x1 * cos - x2 * sin).astype(q_ref.dtype)
        q_ref[:, h * RET_QK_DIM + half:(h + 1) * RET_QK_DIM] = (x2 * cos + x1 * sin).astype(q_ref.dtype)
    acck = _dot_nt(wkt_ref[...], xn)
    reps = crt_ref.shape[1] // LANES
    cbt = jnp.concatenate([cbt_ref[...]] * reps, axis=1)
    sbt = jnp.concatenate([sbt_ref[...]] * reps, axis=1)
    cost = cbt * crt_ref[...] - sbt * srt_ref[...]
    sint = sbt * crt_ref[...] + cbt * srt_ref[...]
    scale = RET_QK_DIM ** -0.5
    for h in range(RET_HEADS):
        x1 = acck[h * RET_QK_DIM:h * RET_QK_DIM + half, :]
        x2 = acck[h * RET_QK_DIM + half:(h + 1) * RET_QK_DIM, :]
        kt_ref[h * RET_QK_DIM:h * RET_QK_DIM + half, :] = ((x1 * cost - x2 * sint) * scale).astype(kt_ref.dtype)
        kt_ref[h * RET_QK_DIM + half:(h + 1) * RET_QK_DIM, :] = ((x2 * cost + x1 * sint) * scale).astype(kt_ref.dtype)
    v_ref[...] = _dot(xn, wv_ref[...]).astype(v_ref.dtype)
    g_ref[...] = _dot(xn, wg_ref[...]).astype(g_ref.dtype)


def _ret_proj(x2d, norm_w, angle, w_q, w_kt, w_v, w_g, seq_len):
    n, d = x2d.shape
    tm = min(TOKEN_TILE, seq_len)
    tiles_per_seq = seq_len // tm
    half = angle.shape[0]
    qk_w = w_q.shape[1]
    row = lambda i: (i, 0)
    col = lambda i: (0, i)
    const = lambda i: (0, 0)
    th_r = jnp.arange(tm, dtype=F32)[:, None] * angle[None, :]
    th_b = (jnp.arange(tiles_per_seq, dtype=F32) * tm)[:, None] * angle[None, :]
    cos_r, sin_r = jnp.cos(th_r), jnp.sin(th_r)
    cos_b, sin_b = jnp.cos(th_b)[:, None, :], jnp.sin(th_b)[:, None, :]
    cos_bt = jnp.broadcast_to(jnp.cos(th_b)[:, :, None], (tiles_per_seq, half, LANES))
    sin_bt = jnp.broadcast_to(jnp.sin(th_b)[:, :, None], (tiles_per_seq, half, LANES))
    base = lambda i: (i % tiles_per_seq, 0, 0)
    blk = (_nbytes((tm, d), F32) + 4 * _nbytes((tm, half), F32) + 2 * _nbytes((half, LANES), F32)
           + 2 * _nbytes(w_q.shape, BF16)
           + 2 * _nbytes(w_v.shape, BF16) + 2 * _nbytes((tm, qk_w), BF16) + 2 * _nbytes((tm, D_INNER), BF16))
    return pl.pallas_call(
        _ret_proj_kernel, grid=(n // tm,),
        in_specs=[pl.BlockSpec((tm, d), row), pl.BlockSpec((1, d), const),
                  pl.BlockSpec((tm, half), const), pl.BlockSpec((tm, half), const),
                  pl.BlockSpec((None, 1, half), base), pl.BlockSpec((None, 1, half), base),
                  pl.BlockSpec((half, tm), const), pl.BlockSpec((half, tm), const),
                  pl.BlockSpec((None, half, LANES), base), pl.BlockSpec((None, half, LANES), base),
                  pl.BlockSpec(w_q.shape, const), pl.BlockSpec(w_kt.shape, const),
                  pl.BlockSpec(w_v.shape, const), pl.BlockSpec(w_g.shape, const)],
        out_specs=[pl.BlockSpec((tm, qk_w), row), pl.BlockSpec((qk_w, tm), col),
                   pl.BlockSpec((tm, D_INNER), row), pl.BlockSpec((tm, D_INNER), row)],
        out_shape=[jax.ShapeDtypeStruct((n, qk_w), BF16), jax.ShapeDtypeStruct((qk_w, n), BF16),
                   jax.ShapeDtypeStruct((n, D_INNER), BF16), jax.ShapeDtypeStruct((n, D_INNER), BF16)],
        compiler_params=_params(("parallel",), blk), name="ret_proj")(
            x2d, norm_w.reshape(1, d), cos_r, sin_r, cos_b, sin_b, cos_r.T, sin_r.T, cos_bt, sin_bt,
            w_q, w_kt, w_v, w_g)


def _s5_taps_kernel(r_ref, a_ref, k_ref):
    k_ref[...] = jnp.dot(r_ref[...], a_ref[...], preferred_element_type=F32,
                         precision=lax.Precision.HIGHEST)


def _s5_taps(rcat, apcat):
    g, m, kk = rcat.shape
    t = apcat.shape[2]
    return pl.pallas_call(
        _s5_taps_kernel, grid=(g,),
        in_specs=[pl.BlockSpec((None, m, kk), lambda i: (i, 0, 0)),
                  pl.BlockSpec((None, kk, t), lambda i: (i, 0, 0))],
        out_specs=pl.BlockSpec((None, m, t), lambda i: (i, 0, 0)),
        out_shape=jax.ShapeDtypeStruct((g, m, t), F32),
        compiler_params=_params(("parallel",), _nbytes((m, kk + t), F32) + _nbytes((kk, t), F32)),
        name="s5_taps")(rcat, apcat)


def _s5_core_kernel(u_ref, k_ref, f_ref, bt_ref, a_ref, ct_ref, sc_ref, y_ref, w_ref, *, n_chunks):
    ts = S5_CHUNK
    h = S5_GROUP
    half = S5_STATE
    rows = u_ref.shape[0]
    u = u_ref[...]

    f1 = f_ref[0]
    f2 = f_ref[1]
    pmat = jnp.concatenate(
        [(f1 * bt_ref[0, hi:hi + 1, :] + f2 * bt_ref[1, hi:hi + 1, :]).astype(BF16) for hi in range(h)], axis=0)
    e = _dot(u, pmat)

    row_idx = lax.broadcasted_iota(jnp.int32, (n_chunks, 2 * half), 0)
    hprev = []
    for b in range(rows // n_chunks):
        acc = e[b * n_chunks:(b + 1) * n_chunks, :]
        d = 1
        lvl = 0
        while d < n_chunks:
            sh = jnp.where(row_idx >= d, pltpu.roll(acc, d, 0), 0.0)
            acc = (acc + sc_ref[2 * lvl:2 * lvl + 1, :] * sh
                   + sc_ref[2 * lvl + 1:2 * lvl + 2, :] * pltpu.roll(sh, half, 1))
            d *= 2
            lvl += 1
        hprev.append(jnp.where(row_idx >= 1, pltpu.roll(acc, 1, 0), 0.0))
    hp = jnp.concatenate(hprev, axis=0).astype(BF16)

    s_idx = lax.broadcasted_iota(jnp.int32, (ts, LANES), 0)
    l_idx = lax.broadcasted_iota(jnp.int32, (ts, LANES), 1)
    keep = (l_idx & (ts - 1)) >= s_idx
    first = lax.broadcasted_iota(jnp.int32, (2 * half, LANES), 1) < ts
    slabs_per_tile = MXU_DIM // LANES
    kdim = h * ts
    for n in range(h * ts // MXU_DIM):
        for j in range(slabs_per_tile):
            m = n * slabs_per_tile + j
            cols = slice(j * LANES, (j + 1) * LANES)
            for hi in range(h):
                kb = jnp.broadcast_to(k_ref[hi * (h // 2) + m:hi * (h // 2) + m + 1, :], (ts, LANES))
                tz = pltpu.roll(kb, 0, 1, stride=1, stride_axis=0)
                w_ref[n, hi * ts:(hi + 1) * ts, cols] = jnp.where(keep, tz, 0.0).astype(BF16)
            k1 = jnp.where(first, jnp.broadcast_to(ct_ref[0, :, 2 * m:2 * m + 1], (2 * half, LANES)),
                           jnp.broadcast_to(ct_ref[0, :, 2 * m + 1:2 * m + 2], (2 * half, LANES)))
            k2 = jnp.where(first, jnp.broadcast_to(ct_ref[1, :, 2 * m:2 * m + 1], (2 * half, LANES)),
                           jnp.broadcast_to(ct_ref[1, :, 2 * m + 1:2 * m + 2], (2 * half, LANES)))
            w_ref[n, kdim:kdim + 2 * half, cols] = (k1 * a_ref[0] + k2 * a_ref[1]).astype(BF16)
        y = _dot(u, w_ref[n, 0:kdim, :]) + _dot(hp, w_ref[n, kdim:kdim + 2 * half, :])
        y_ref[:, n * MXU_DIM:(n + 1) * MXU_DIM] = y.astype(y_ref.dtype)


def _s5_core(u2, taps2, ftab, bttab, atab, cttab, sc, n_chunks):
    rows = u2.shape[0]
    g = taps2.shape[0]
    w = S5_GROUP * S5_CHUNK
    tabs = [taps2, ftab, bttab, atab, cttab, sc]
    blk = 2 * _nbytes((rows, w), BF16) + sum(_nbytes(t.shape[1:], F32) for t in tabs)
    wshape = (w // MXU_DIM, w + 2 * S5_STATE, MXU_DIM)

    def tab_spec(t):
        nd = t.ndim - 1
        return pl.BlockSpec((None,) + t.shape[1:], lambda i: (i,) + (0,) * nd)

    return pl.pallas_call(
        functools.partial(_s5_core_kernel, n_chunks=n_chunks), grid=(g,),
        in_specs=[pl.BlockSpec((rows, w), lambda i: (0, i))] + [tab_spec(t) for t in tabs],
        out_specs=pl.BlockSpec((rows, w), lambda i: (0, i)),
        out_shape=jax.ShapeDtypeStruct((rows, g * w), BF16),
        scratch_shapes=[pltpu.VMEM(wshape, BF16)],
        compiler_params=_params(("parallel",), blk, _nbytes(wshape, BF16)),
        name="s5_core")(u2, *tabs)


def _s5_prepare(lam_re, lam_im, log_dt, b_re, b_im, c_re, c_im, n_chunks):
    ts = S5_CHUNK
    assert 2 * ts == LANES and 2 * S5_STATE == LANES
    dt = jnp.exp(log_dt.astype(F32))[:, None]
    lr = jnp.minimum(lam_re.astype(F32), -1e-4)
    li = lam_im.astype(F32)
    mag = jnp.exp(lr * dt)
    ab_re = mag * jnp.cos(li * dt)
    ab_im = mag * jnp.sin(li * dt)
    den = lr * lr + li * li
    nr = ab_re - 1.0
    coef_re = (nr * lr + ab_im * li) / den
    coef_im = (ab_im * lr - nr * li) / den
    br = b_re.astype(F32)
    bi = b_im.astype(F32)
    bb_re = coef_re[..., None] * br - coef_im[..., None] * bi
    bb_im = coef_re[..., None] * bi + coef_im[..., None] * br
    cr = c_re.astype(F32)
    ci = c_im.astype(F32)
    g = lr.shape[0]
    j = jnp.arange(ts + 1, dtype=F32)
    pmag = jnp.exp((lr * dt)[..., None] * j)
    ang = (li * dt)[..., None] * j
    pw_re = pmag * jnp.cos(ang)
    pw_im = pmag * jnp.sin(ang)
    bbt_re = bb_re.transpose(0, 2, 1)
    bbt_im = bb_im.transpose(0, 2, 1)
    r_re = cr[:, None] * bbt_re[:, :, None] - ci[:, None] * bbt_im[:, :, None]
    r_im = cr[:, None] * bbt_im[:, :, None] + ci[:, None] * bbt_re[:, :, None]
    rcat = jnp.concatenate([r_re, -r_im], axis=-1).reshape(g, S5_GROUP * S5_GROUP, 2 * S5_STATE)
    apcat = jnp.concatenate([pw_re[..., :ts], pw_im[..., :ts]], axis=1)
    fr = pw_re[..., ts - 1::-1].transpose(0, 2, 1)
    fi = pw_im[..., ts - 1::-1].transpose(0, 2, 1)
    ftab = jnp.stack([jnp.concatenate([fr, fr], -1), jnp.concatenate([-fi, fi], -1)], axis=1)
    bttab = jnp.stack([jnp.concatenate([bbt_re, bbt_im], -1), jnp.concatenate([bbt_im, bbt_re], -1)], axis=1)
    wr = pw_re[..., 1:]
    wi = pw_im[..., 1:]
    a1 = jnp.concatenate([wr, -wi], axis=1)
    a2 = jnp.concatenate([-wi, -wr], axis=1)
    atab = jnp.stack([jnp.concatenate([a1, a1], -1), jnp.concatenate([a2, a2], -1)], axis=1)
    crt = cr.transpose(0, 2, 1)
    cit = ci.transpose(0, 2, 1)
    cttab = jnp.stack([jnp.concatenate([crt, crt], 1), jnp.concatenate([cit, cit], 1)], axis=1)
    rows = []
    d = 1
    while d < n_chunks:
        ex = float(ts * d)
        m = jnp.exp(lr * dt * ex)
        a_r = m * jnp.cos(li * dt * ex)
        a_i = m * jnp.sin(li * dt * ex)
        rows.append(jnp.concatenate([a_r, a_r], axis=-1))
        rows.append(jnp.concatenate([-a_i, a_i], axis=-1))
        d *= 2
    while len(rows) % SUBLANES or not rows:
        rows.append(jnp.zeros((g, 2 * S5_STATE), F32))
    sc = jnp.stack(rows, axis=1)
    return rcat, apcat, ftab, bttab, atab, cttab, sc


def _ssd_kernel(xbc_ref, z_ref, dt_ref, shift_ref, tri_ref, exp_ref, cw_ref, cb_ref, dtb_ref, a_ref,
                dsk_ref, nw_ref, y_ref, xcat_ref, st_ref, yg_ref):
    t = CHUNK
    c = pl.program_id(0)

    @pl.when(c == 0)
    def _():
        xcat_ref[:, 0:t, :] = jnp.zeros((xcat_ref.shape[0], t, SSD_CONV_DIM), BF16)
        st_ref[...] = jnp.zeros_like(st_ref)

    @pl.when(c != 0)
    def _():
        xcat_ref[:, 0:t, :] = xcat_ref[:, t:2 * t, :]

    for b in range(xbc_ref.shape[0]):
        _ssd_chunk(xbc_ref.at[b], z_ref.at[b], dt_ref.at[b], shift_ref, tri_ref, exp_ref, cw_ref, cb_ref,
                   dtb_ref, a_ref, dsk_ref, nw_ref, y_ref.at[b], xcat_ref.at[b],
                   st_ref.at[pl.ds(b * SSD_GROUPS, SSD_GROUPS)], yg_ref.at[b])


def _ssd_chunk(xbc_ref, z_ref, dt_ref, shift_ref, tri_ref, exp_ref, cw_ref, cb_ref, dtb_ref, a_ref,
               dsk_ref, nw_ref, y_ref, xcat_ref, st_ref, yg_ref):
    t = CHUNK
    xcat_ref[t:2 * t, :] = xbc_ref[...]
    xsh = _dot(shift_ref[...], xcat_ref[...])
    acc = cb_ref[...] + cw_ref[SSD_CONV - 1:SSD_CONV, :] * xbc_ref[...].astype(F32)
    for k in range(SSD_CONV - 1):
        acc = acc + cw_ref[k:k + 1, :] * xsh[k * t:(k + 1) * t, :]
    xc = _silu(acc)
    gn = SSD_GROUPS * SSD_STATE
    xs = xc[:, :SSD_WIDTH]
    bm = xc[:, SSD_WIDTH:SSD_WIDTH + gn]
    cm = xc[:, SSD_WIDTH + gn:]

    xr = dt_ref[...] + dtb_ref[...]
    dt = jnp.maximum(xr, 0.0) + jnp.log(1.0 + jnp.exp(-jnp.abs(xr)))
    da = dt * a_ref[...]
    cs3 = _dot(tri_ref[...], jnp.concatenate(_split3(da), axis=1))
    a_cs = cs3[:, :LANES] + cs3[:, LANES:2 * LANES] + cs3[:, 2 * LANES:]
    expand = exp_ref[...]
    dt_e = _dot(jnp.concatenate(_split3(dt)[:2], axis=1), expand)
    acs_e = _dot(jnp.concatenate(_split3(a_cs)[:2], axis=1), expand)
    last_e = acs_e[t - 1:t, :]
    xdt = xs * dt_e
    xdt_b = xdt.astype(BF16)
    xd_b = (xdt * jnp.exp(last_e - acs_e)).astype(BF16)
    exp_acs = jnp.exp(acs_e)
    chunk_decay = jnp.exp(last_e)
    a_cs_t = a_cs.T
    r_i = lax.broadcasted_iota(jnp.int32, (t, t), 0)
    c_i = lax.broadcasted_iota(jnp.int32, (t, t), 1)
    causal = r_i >= c_i
    lane = lax.broadcasted_iota(jnp.int32, (t, LANES), 1)
    gw = SSD_REP * SSD_HEAD_DIM
    for g in range(SSD_GROUPS):
        bg = bm[:, g * SSD_STATE:(g + 1) * SSD_STATE]
        cg = cm[:, g * SSD_STATE:(g + 1) * SSD_STATE].astype(BF16)
        gmat = _dot_nt(cg, bg.astype(BF16))
        s_prev = st_ref[g]
        y_off = _dot(cg, s_prev.astype(BF16)) * exp_acs[:, g * gw:(g + 1) * gw]
        st_ref[g] = (s_prev * chunk_decay[:, g * gw:(g + 1) * gw]
                     + _dot(bg.T.astype(BF16), xd_b[:, g * gw:(g + 1) * gw]))
        for pr in range(SSD_REP // 2):
            ms = []
            for hh in range(2):
                hd = g * SSD_REP + 2 * pr + hh
                seg = a_cs[:, hd:hd + 1] - a_cs_t[hd:hd + 1, :]
                ms.append((gmat * jnp.exp(jnp.where(causal, seg, -jnp.inf))).astype(BF16))
            lo = g * gw + pr * LANES
            x2 = xdt_b[:, lo:lo + LANES]
            rhs = jnp.concatenate([jnp.where(lane < SSD_HEAD_DIM, x2, jnp.zeros_like(x2)),
                                   jnp.where(lane >= SSD_HEAD_DIM, x2, jnp.zeros_like(x2))], axis=0)
            y_d = _dot(jnp.concatenate(ms, axis=1), rhs)
            y = y_d + y_off[:, pr * LANES:(pr + 1) * LANES] + xs[:, lo:lo + LANES] * dsk_ref[:, lo:lo + LANES]
            yg_ref[:, lo:lo + LANES] = y * _silu(z_ref[:, lo:lo + LANES].astype(F32))
    for g in range(SSD_GROUPS):
        yg = yg_ref[:, g * gw:(g + 1) * gw]
        ms_ = jnp.mean(yg * yg, axis=-1, keepdims=True)
        y_ref[:, g * gw:(g + 1) * gw] = (yg * lax.rsqrt(ms_ + NORM_EPS)
                                         * nw_ref[:, g * gw:(g + 1) * gw]).astype(y_ref.dtype)


def _ssd(xbc, z, dt, conv_w, conv_b, dt_bias, a_neg, d_e, norm_w, batch, n_chunks):
    t = CHUNK
    n = xbc.shape[0]
    seq_len = n // batch
    row = lambda c: (0, c, 0)
    const = lambda c: (0, 0)
    xbc = xbc.reshape(batch, seq_len, SSD_CONV_DIM)
    z = z.reshape(batch, seq_len, z.shape[1])
    dt = dt.reshape(batch, seq_len, LANES)
    rr = jnp.arange((SSD_CONV - 1) * t)
    src = t + rr % t - (SSD_CONV - 1) + rr // t
    shift = (jnp.arange(2 * t)[None, :] == src[:, None]).astype(BF16)
    tri = (jnp.arange(t)[:, None] >= jnp.arange(t)[None, :]).astype(BF16)
    expand = (jnp.arange(SSD_WIDTH)[None, :] // SSD_HEAD_DIM
              == (jnp.arange(2 * LANES) % LANES)[:, None]).astype(BF16)
    blk = (batch * (_nbytes((t, SSD_CONV_DIM), BF16) + 2 * _nbytes((t, SSD_WIDTH), BF16)
                    + _nbytes((t, LANES), F32))
           + _nbytes(shift.shape, BF16) + _nbytes(tri.shape, BF16) + _nbytes(expand.shape, BF16)
           + 6 * _nbytes((SUBLANES, SSD_CONV_DIM), F32))
    st_shape = (batch * SSD_GROUPS, SSD_STATE, SSD_REP * SSD_HEAD_DIM)
    scratch = (_nbytes((batch, 2 * t, SSD_CONV_DIM), BF16) + _nbytes(st_shape, F32)
               + _nbytes((batch, t, SSD_WIDTH), F32))
    out = pl.pallas_call(
        _ssd_kernel, grid=(n_chunks,),
        in_specs=[pl.BlockSpec((batch, t, SSD_CONV_DIM), row),
                  pl.BlockSpec((batch, t, SSD_WIDTH), lambda c: (0, c, 1)),
                  pl.BlockSpec((batch, t, LANES), row),
                  pl.BlockSpec(shift.shape, const), pl.BlockSpec(tri.shape, const),
                  pl.BlockSpec(expand.shape, const),
                  pl.BlockSpec(conv_w.shape, const), pl.BlockSpec(conv_b.shape, const),
                  pl.BlockSpec(dt_bias.shape, const), pl.BlockSpec(a_neg.shape, const),
                  pl.BlockSpec(d_e.shape, const), pl.BlockSpec(norm_w.shape, const)],
        out_specs=pl.BlockSpec((batch, t, SSD_WIDTH), row),
        out_shape=jax.ShapeDtypeStruct((batch, seq_len, SSD_WIDTH), BF16),
        scratch_shapes=[pltpu.VMEM((batch, 2 * t, SSD_CONV_DIM), BF16),
                        pltpu.VMEM(st_shape, F32),
                        pltpu.VMEM((batch, t, SSD_WIDTH), F32)],
        compiler_params=_params(("arbitrary",), blk, scratch),
        name="ssd")(xbc, z, dt, shift, tri, expand, conv_w, conv_b, dt_bias, a_neg, d_e, norm_w)
    return out.reshape(n, SSD_WIDTH)


def _post0_kernel(ys_ref, u_ref, z_ref, yb_ref, x_ref, d_ref, gw_ref, gb_ref, wa_ref, wb_ref, o_ref):
    u = u_ref[...].astype(F32)
    v = ys_ref[...].astype(F32) + d_ref[...] * u
    y = 0.5 * v * (1.0 + jnp.tanh(math.sqrt(2.0 / math.pi) * (v + 0.044715 * (v * v * v))))
    gl = _dot(y.astype(BF16), gw_ref[...]) + gb_ref[...]
    ya = y * jax.nn.sigmoid(gl) * _silu(z_ref[...].astype(F32))
    out = _dot(ya.astype(BF16), wa_ref[...]) + _dot(yb_ref[...].astype(BF16), wb_ref[...])
    o_ref[...] = x_ref[...] + out


def _post0(ys, u, z, yb, x2d, s5_d, glu_w, glu_b, w_a, w_b):
    n, d = x2d.shape
    tm = min(TOKEN_TILE, n)
    row = lambda i: (i, 0)
    const = lambda i: (0, 0)
    blk = (3 * _nbytes((tm, d), BF16) + 3 * _nbytes((tm, d), F32) + 3 * _nbytes((d, d), BF16)
           + 2 * _nbytes((SUBLANES, d), F32))
    return pl.pallas_call(
        _post0_kernel, grid=(n // tm,),
        in_specs=[pl.BlockSpec((tm, d), row), pl.BlockSpec((tm, d), row), pl.BlockSpec((tm, d), row),
                  pl.BlockSpec((tm, d), row), pl.BlockSpec((tm, d), row),
                  pl.BlockSpec((1, d), const), pl.BlockSpec(glu_w.shape, const), pl.BlockSpec((1, d), const),
                  pl.BlockSpec(w_a.shape, const), pl.BlockSpec(w_b.shape, const)],
        out_specs=pl.BlockSpec((tm, d), row),
        out_shape=jax.ShapeDtypeStruct((n, d), F32),
        compiler_params=_params(("parallel",), blk), name="post0")(
            ys, u, z, yb, x2d, s5_d, glu_w, glu_b, w_a, w_b)


def _ret_kernel(q_ref, kt_ref, v_ref, g_ref, h_ref, dmat_ref, qd_ref, kdt_ref, cd_ref, gw_ref, gb_ref,
                wo_ref, fw_ref, o_ref, st_ref, y_ref, *, final_norm):
    c = pl.program_id(1)

    @pl.when(c == 0)
    def _():
        st_ref[...] = jnp.zeros_like(st_ref)

    for h in range(RET_HEADS):
        qh = q_ref[:, h * RET_QK_DIM:(h + 1) * RET_QK_DIM]
        kth = kt_ref[h * RET_QK_DIM:(h + 1) * RET_QK_DIM, :]
        vh = v_ref[:, h * RET_V_DIM:(h + 1) * RET_V_DIM]
        scores = _dot(qh, kth) * dmat_ref[h]
        inner = _dot(scores.astype(BF16), vh)
        s_prev = st_ref[h]
        qdec = (qh.astype(F32) * qd_ref[h]).astype(BF16)
        cross = _dot(qdec, s_prev.astype(BF16))
        kdec = (kth.astype(F32) * kdt_ref[h]).astype(BF16)
        st_ref[h] = s_prev * cd_ref[h] + _dot(kdec, vh)
        o = inner + cross
        mu = jnp.mean(o, axis=-1, keepdims=True)
        oc = o - mu
        var = jnp.mean(oc * oc, axis=-1, keepdims=True)
        sl = slice(h * RET_V_DIM, (h + 1) * RET_V_DIM)
        on = oc * lax.rsqrt(var + NORM_EPS) * gw_ref[:, sl] + gb_ref[:, sl]
        y_ref[:, sl] = (_silu(g_ref[:, sl].astype(F32)) * on).astype(y_ref.dtype)
    hn = h_ref[...] + _dot(y_ref[...], wo_ref[...])
    if final_norm:
        ms = jnp.mean(hn * hn, axis=-1, keepdims=True)
        hn = hn * lax.rsqrt(ms + NORM_EPS) * fw_ref[...]
    o_ref[...] = hn


def _retention(q, kt, v, g, h2d, dmat, qd, kdt, cd, gn_w, gn_b, w_out, final_w, batch, seq_len, final_norm):
    rt = dmat.shape[1]
    n_chunks = seq_len // rt
    n, d = h2d.shape
    row = lambda b, c: (b * n_chunks + c, 0)
    col = lambda b, c: (0, b * n_chunks + c)
    c2 = lambda b, c: (0, 0)
    c3 = lambda b, c: (0, 0, 0)
    blk = (2 * _nbytes((rt, D_MODEL), BF16) + 2 * _nbytes((rt, D_INNER), BF16) + 2 * _nbytes((rt, d), F32)
           + _nbytes(dmat.shape, F32) + 2 * _nbytes(qd.shape, F32) + _nbytes(cd.shape, F32)
           + _nbytes(w_out.shape, BF16) + 3 * _nbytes((SUBLANES, D_INNER), F32))
    scratch = _nbytes((RET_HEADS, RET_QK_DIM, RET_V_DIM), F32) + _nbytes((rt, D_INNER), BF16)
    return pl.pallas_call(
        functools.partial(_ret_kernel, final_norm=final_norm), grid=(batch, n_chunks),
        in_specs=[pl.BlockSpec((rt, D_MODEL), row), pl.BlockSpec((D_MODEL, rt), col),
                  pl.BlockSpec((rt, D_INNER), row), pl.BlockSpec((rt, D_INNER), row),
                  pl.BlockSpec((rt, d), row),
                  pl.BlockSpec(dmat.shape, c3), pl.BlockSpec(qd.shape, c3), pl.BlockSpec(kdt.shape, c3),
                  pl.BlockSpec(cd.shape, c3), pl.BlockSpec((1, D_INNER), c2), pl.BlockSpec((1, D_INNER), c2),
                  pl.BlockSpec(w_out.shape, c2), pl.BlockSpec((1, d), c2)],
        out_specs=pl.BlockSpec((rt, d), row),
        out_shape=jax.ShapeDtypeStruct((n, d), F32),
        scratch_shapes=[pltpu.VMEM((RET_HEADS, RET_QK_DIM, RET_V_DIM), F32),
                        pltpu.VMEM((rt, D_INNER), BF16)],
        compiler_params=_params(("arbitrary", "arbitrary"), blk, scratch),
        name="retention")(q, kt, v, g, h2d, dmat, qd, kdt, cd, gn_w, gn_b, w_out, final_w)


def _s5_ssd_layer(h2d, norm_w, batch, seq_len, w_in, lam_re, lam_im, log_dt, b_re, b_im, c_re, c_im,
                  s5_d, glu_w, glu_b, conv_w, conv_b, dt_bias, a_log, ssd_d, ssd_norm_w, w_out):
    n = batch * seq_len
    o1 = D_INNER
    o2 = o1 + S5_WIDTH
    o3 = o2 + SSD_CONV_DIM
    w_z = w_in[:, :o1].astype(BF16)
    w_u = w_in[:, o1:o2].astype(BF16)
    w_x = w_in[:, o2:o3].astype(BF16)
    w_dt = jnp.pad(w_in[:, o3:], ((0, 0), (0, LANES - SSD_HEADS))).astype(BF16)
    z, u, xbc, dt = _norm_proj(h2d, norm_w, [w_z, w_u, w_x, w_dt], [BF16, BF16, BF16, F32])

    ts = S5_CHUNK
    nc_s = seq_len // ts
    rcat, apcat, ftab, bttab, atab, cttab, sc = _s5_prepare(lam_re, lam_im, log_dt, b_re, b_im, c_re, c_im, nc_s)
    taps2 = _s5_taps(rcat, apcat).reshape(S5_GROUPS, S5_GROUP * S5_GROUP // 2, 2 * ts)
    u2 = u.reshape(batch * nc_s, ts, S5_WIDTH).transpose(0, 2, 1).reshape(batch * nc_s, S5_WIDTH * ts)
    y2 = _s5_core(u2, taps2, ftab, bttab, atab, cttab, sc, nc_s)
    ys = y2.reshape(batch * nc_s, S5_WIDTH, ts).transpose(0, 2, 1).reshape(n, S5_WIDTH)

    pad_h = LANES - SSD_HEADS
    cw = jnp.pad(conv_w.astype(F32), ((0, SUBLANES - SSD_CONV), (0, 0)))
    cb = conv_b.astype(F32).reshape(1, SSD_CONV_DIM)
    dtb = jnp.pad(dt_bias.astype(F32), (0, pad_h)).reshape(1, LANES)
    a_neg = jnp.pad(-jnp.exp(a_log.astype(F32)), (0, pad_h)).reshape(1, LANES)
    d_e = jnp.repeat(ssd_d.astype(F32), SSD_HEAD_DIM).reshape(1, SSD_WIDTH)
    nw = ssd_norm_w.astype(F32).reshape(1, SSD_WIDTH)
    yb = _ssd(xbc, z, dt, cw, cb, dtb, a_neg, d_e, nw, batch, seq_len // CHUNK)

    return _post0(ys, u, z, yb, h2d, s5_d.astype(F32).reshape(1, S5_WIDTH), glu_w.astype(BF16),
                  glu_b.astype(F32).reshape(1, S5_WIDTH), w_out[:S5_WIDTH].astype(BF16),
                  w_out[S5_WIDTH:].astype(BF16))


def _retention_layer(h2d, norm_w, batch, seq_len, w_in, gn_w, gn_b, w_out, final_w, final_norm):
    qk_w = RET_HEADS * RET_QK_DIM
    half = RET_QK_DIM // 2

    def deinterleave(w):
        return w.reshape(D_MODEL, RET_HEADS, half, 2).transpose(0, 1, 3, 2).reshape(D_MODEL, qk_w)

    w_q = deinterleave(w_in[:, :qk_w]).astype(BF16)
    w_kt = deinterleave(w_in[:, qk_w:2 * qk_w]).T.astype(BF16)
    w_v = w_in[:, 2 * qk_w:2 * qk_w + D_INNER].astype(BF16)
    w_g = w_in[:, 2 * qk_w + D_INNER:].astype(BF16)
    angle = 1.0 / (ROPE_BASE ** jnp.linspace(0.0, 1.0, half, dtype=F32))
    q, kt, v, g = _ret_proj(h2d, norm_w, angle, w_q, w_kt, w_v, w_g, seq_len)

    rt = min(RET_CHUNK, seq_len)
    log_gamma = jnp.log(1.0 - 2.0 ** (-5.0 - jnp.arange(RET_HEADS, dtype=F32)))
    p = jnp.arange(rt, dtype=F32)
    rel = p[:, None] - p[None, :]
    dmat = jnp.where(rel >= 0, jnp.exp(log_gamma[:, None, None] * jnp.maximum(rel, 0.0)), 0.0)
    qd = jnp.broadcast_to(jnp.exp(log_gamma[:, None] * (p[None, :] + 1.0))[..., None],
                          (RET_HEADS, rt, RET_QK_DIM))
    kdt = jnp.broadcast_to(jnp.exp(log_gamma[:, None] * (rt - 1.0 - p)[None, :])[:, None, :],
                           (RET_HEADS, RET_QK_DIM, rt))
    cd = jnp.broadcast_to(jnp.exp(log_gamma * rt)[:, None, None], (RET_HEADS, 1, RET_V_DIM))
    return _retention(q, kt, v, g, h2d, dmat, qd, kdt, cd, gn_w.astype(F32).reshape(1, D_INNER),
                      gn_b.astype(F32).reshape(1, D_INNER), w_out.astype(BF16),
                      final_w.astype(F32).reshape(1, D_MODEL), batch, seq_len, final_norm)


def kernel(x, layer_norm_w, ab_w_in, s5_lam_re, s5_lam_im, s5_log_dt, s5_b_re, s5_b_im, s5_c_re, s5_c_im,
           s5_d, s5_glu_w, s5_glu_b, ssd_conv_w, ssd_conv_b, ssd_dt_bias, ssd_a_log, ssd_d, ssd_norm_w,
           ab_w_out, ret_w_in, ret_gn_w, ret_gn_b, ret_w_out, final_norm_w):
    batch, seq_len, d = x.shape
    depth = layer_norm_w.shape[0]
    assert depth % 2 == 0 and seq_len % CHUNK == 0 and d == D_MODEL
    h = x.astype(F32).reshape(batch * seq_len, d)
    for i in range(depth):
        j = i // 2
        if i % 2 == 0:
            h = _s5_ssd_layer(h, layer_norm_w[i], batch, seq_len, ab_w_in[j], s5_lam_re[j], s5_lam_im[j],
                              s5_log_dt[j], s5_b_re[j], s5_b_im[j], s5_c_re[j], s5_c_im[j], s5_d[j],
                              s5_glu_w[j], s5_glu_b[j], ssd_conv_w[j], ssd_conv_b[j], ssd_dt_bias[j],
                              ssd_a_log[j], ssd_d[j], ssd_norm_w[j], ab_w_out[j])
        else:
            h = _retention_layer(h, layer_norm_w[i], batch, seq_len, ret_w_in[j], ret_gn_w[j], ret_gn_b[j],
                                 ret_w_out[j], final_norm_w, final_norm=(i == depth - 1))
    return h.reshape(batch, seq_len, d).astype(x.dtype)
```

```python
import functools
import math

import jax
import jax.numpy as jnp
from jax import lax
from jax.experimental import pallas as pl
from jax.experimental.pallas import tpu as pltpu

F32 = jnp.float32
BF16 = jnp.bfloat16

D_MODEL = 1024
D_INNER = 2048
CHUNK = 128
NORM_EPS = 1e-6
S5_WIDTH = 1024
S5_GROUP = 16
S5_GROUPS = 64
S5_STATE = 64
SSD_WIDTH = 1024
SSD_HEAD_DIM = 64
SSD_HEADS = 16
SSD_GROUPS = 2
SSD_REP = 8
SSD_STATE = 128
SSD_CONV = 4
SSD_CONV_DIM = 1536
RET_HEADS = 4
RET_QK_DIM = 256
RET_V_DIM = 512
ROPE_BASE = 10000.0

LANES = 128
SUBLANES = 8
MXU_DIM = 256
VMEM_CAP = 60 * 1024 * 1024
TOKEN_TILE = 512
S5_CHUNK = 64
RET_CHUNK = 256
S5_SC_ROWS = 2 * SUBLANES
S5_TAB_F = 0
S5_TAB_B = S5_TAB_F + 2 * S5_CHUNK
S5_TAB_A = S5_TAB_B + 2 * S5_GROUP
S5_TAB_SC = S5_TAB_A + 4 * S5_STATE
S5_TAB_C = S5_TAB_SC + S5_SC_ROWS
S5_TAB_ROWS = S5_TAB_C + 4 * S5_STATE


def _params(semantics, block_bytes, scratch_bytes=0):
    need = 2 * block_bytes + scratch_bytes + 16 * 1024 * 1024
    return pltpu.CompilerParams(dimension_semantics=semantics,
                                vmem_limit_bytes=int(min(need, VMEM_CAP)))


def _nbytes(shape, dtype):
    return math.prod(shape) * jnp.dtype(dtype).itemsize


def _silu(x):
    return x * jax.nn.sigmoid(x)


def _split3(x):
    hi = x.astype(BF16)
    r1 = x - hi.astype(F32)
    mid = r1.astype(BF16)
    lo = (r1 - mid.astype(F32)).astype(BF16)
    return hi, mid, lo


def _dot(a, b):
    return jnp.dot(a, b, preferred_element_type=F32)


def _dot_nt(a, b):
    return lax.dot_general(a, b, (((1,), (1,)), ((), ())), preferred_element_type=F32)


def _rmsnorm_bf16(x, w):
    ms = jnp.mean(x * x, axis=-1, keepdims=True)
    return (x * lax.rsqrt(ms + NORM_EPS) * w).astype(BF16)


def _norm_proj_kernel(x_ref, nw_ref, w_ref, *o_refs, splits):
    xn = _rmsnorm_bf16(x_ref[...], nw_ref[...])
    for (lo, hi), o_ref in zip(splits, o_refs):
        o_ref[...] = _dot(xn, w_ref[:, lo:hi]).astype(o_ref.dtype)


def _norm_proj(x2d, norm_w, w, splits, out_dtypes):
    n, d = x2d.shape
    tm = min(TOKEN_TILE, n)
    row = lambda i: (i, 0)
    const = lambda i: (0, 0)
    blk = _nbytes((tm, d), F32) + _nbytes(w.shape, w.dtype)
    out_specs, out_shape = [], []
    for (lo, hi), dt in zip(splits, out_dtypes):
        assert lo % LANES == 0 and hi % LANES == 0
        out_specs.append(pl.BlockSpec((tm, hi - lo), row))
        out_shape.append(jax.ShapeDtypeStruct((n, hi - lo), dt))
        blk += _nbytes((tm, hi - lo), dt)
    return pl.pallas_call(
        functools.partial(_norm_proj_kernel, splits=tuple(splits)), grid=(n // tm,),
        in_specs=[pl.BlockSpec((tm, d), row), pl.BlockSpec((1, d), const), pl.BlockSpec(w.shape, const)],
        out_specs=out_specs, out_shape=out_shape,
        compiler_params=_params(("parallel",), blk), name="norm_proj")(x2d, norm_w.reshape(1, d), w)


def _ret_proj_kernel(x_ref, nw_ref, cr_ref, sr_ref, cb_ref, sb_ref, crt_ref, srt_ref, cbt_ref, sbt_ref,
                     w_ref, wkt_ref, q_ref, kt_ref, v_ref, g_ref):
    xn = _rmsnorm_bf16(x_ref[...], nw_ref[...])
    qk_w = RET_HEADS * RET_QK_DIM
    tm = x_ref.shape[0]
    accq = _dot(xn, w_ref[:, 0:qk_w])
    cos = cb_ref[...] * cr_ref[...] - sb_ref[...] * sr_ref[...]
    sin = sb_ref[...] * cr_ref[...] + cb_ref[...] * sr_ref[...]
    even_l = (lax.broadcasted_iota(jnp.int32, (tm, RET_QK_DIM), 1) & 1) == 0
    for h in range(RET_HEADS):
        xh = accq[:, h * RET_QK_DIM:(h + 1) * RET_QK_DIM]
        partner = jnp.where(even_l, pltpu.roll(xh, RET_QK_DIM - 1, 1), pltpu.roll(xh, 1, 1))
        q_ref[:, h * RET_QK_DIM:(h + 1) * RET_QK_DIM] = (xh * cos + partner * sin).astype(q_ref.dtype)
    acck = _dot_nt(wkt_ref[...], xn)
    reps = tm // LANES
    cbt = jnp.concatenate([cbt_ref[...]] * reps, axis=1)
    sbt = jnp.concatenate([sbt_ref[...]] * reps, axis=1)
    cost = cbt * crt_ref[...] - sbt * srt_ref[...]
    sint = sbt * crt_ref[...] + cbt * srt_ref[...]
    even_r = (lax.broadcasted_iota(jnp.int32, (RET_QK_DIM, tm), 0) & 1) == 0
    scale = RET_QK_DIM ** -0.5
    for h in range(RET_HEADS):
        xh = acck[h * RET_QK_DIM:(h + 1) * RET_QK_DIM, :]
        partner = jnp.where(even_r, pltpu.roll(xh, RET_QK_DIM - 1, 0), pltpu.roll(xh, 1, 0))
        kt_ref[h * RET_QK_DIM:(h + 1) * RET_QK_DIM, :] = ((xh * cost + partner * sint) * scale).astype(kt_ref.dtype)
    v_ref[...] = _dot(xn, w_ref[:, 2 * qk_w:2 * qk_w + D_INNER]).astype(v_ref.dtype)
    g_ref[...] = _dot(xn, w_ref[:, 2 * qk_w + D_INNER:2 * qk_w + 2 * D_INNER]).astype(g_ref.dtype)


def _ret_proj(x2d, norm_w, angle, w, w_kt, seq_len):
    n, d = x2d.shape
    tm = min(TOKEN_TILE, seq_len)
    tiles_per_seq = seq_len // tm
    dk = RET_QK_DIM
    qk_w = RET_HEADS * dk
    row = lambda i: (i, 0)
    col = lambda i: (0, i)
    const = lambda i: (0, 0)
    angle2 = jnp.repeat(angle, 2)
    sign = jnp.where(jnp.arange(dk) % 2 == 0, -1.0, 1.0).astype(F32)
    th_r = jnp.arange(tm, dtype=F32)[:, None] * angle2[None, :]
    th_b = (jnp.arange(tiles_per_seq, dtype=F32) * tm)[:, None] * angle2[None, :]
    cos_r, sin_r = jnp.cos(th_r), jnp.sin(th_r) * sign
    cos_b, sin_b = jnp.cos(th_b), jnp.sin(th_b) * sign
    cos_bt = jnp.broadcast_to(cos_b[:, :, None], (tiles_per_seq, dk, LANES))
    sin_bt = jnp.broadcast_to(sin_b[:, :, None], (tiles_per_seq, dk, LANES))
    base = lambda i: (i % tiles_per_seq, 0, 0)
    blk = (_nbytes((tm, d), F32) + 4 * _nbytes((tm, dk), F32) + 2 * _nbytes((dk, LANES), F32)
           + _nbytes(w.shape, BF16) + _nbytes(w_kt.shape, BF16)
           + 2 * _nbytes((tm, qk_w), BF16) + 2 * _nbytes((tm, D_INNER), BF16))
    return pl.pallas_call(
        _ret_proj_kernel, grid=(n // tm,),
        in_specs=[pl.BlockSpec((tm, d), row), pl.BlockSpec((1, d), const),
                  pl.BlockSpec((tm, dk), const), pl.BlockSpec((tm, dk), const),
                  pl.BlockSpec((None, 1, dk), base), pl.BlockSpec((None, 1, dk), base),
                  pl.BlockSpec((dk, tm), const), pl.BlockSpec((dk, tm), const),
                  pl.BlockSpec((None, dk, LANES), base), pl.BlockSpec((None, dk, LANES), base),
                  pl.BlockSpec(w.shape, const), pl.BlockSpec(w_kt.shape, const)],
        out_specs=[pl.BlockSpec((tm, qk_w), row), pl.BlockSpec((qk_w, tm), col),
                   pl.BlockSpec((tm, D_INNER), row), pl.BlockSpec((tm, D_INNER), row)],
        out_shape=[jax.ShapeDtypeStruct((n, qk_w), BF16), jax.ShapeDtypeStruct((qk_w, n), BF16),
                   jax.ShapeDtypeStruct((n, D_INNER), BF16), jax.ShapeDtypeStruct((n, D_INNER), BF16)],
        compiler_params=_params(("parallel",), blk), name="ret_proj")(
            x2d, norm_w.reshape(1, d), cos_r, sin_r, cos_b[:, None, :], sin_b[:, None, :],
            cos_r.T, sin_r.T, cos_bt, sin_bt, w, w_kt)


def _s5_taps_kernel(r_ref, a_ref, k_ref):
    k_ref[...] = jnp.dot(r_ref[...], a_ref[...], preferred_element_type=F32,
                         precision=lax.Precision.HIGHEST)


def _s5_taps(rcat, apcat):
    g, m, kk = rcat.shape
    t = apcat.shape[2]
    return pl.pallas_call(
        _s5_taps_kernel, grid=(g,),
        in_specs=[pl.BlockSpec((None, m, kk), lambda i: (i, 0, 0)),
                  pl.BlockSpec((None, kk, t), lambda i: (i, 0, 0))],
        out_specs=pl.BlockSpec((None, m, t), lambda i: (i, 0, 0)),
        out_shape=jax.ShapeDtypeStruct((g, m, t), F32),
        compiler_params=_params(("parallel",), _nbytes((m, kk + t), F32) + _nbytes((kk, t), F32)),
        name="s5_taps")(rcat, apcat)


def _s5_core_kernel(u_ref, k_ref, tab_ref, y_ref, w_ref, *, n_chunks):
    ts = S5_CHUNK
    h = S5_GROUP
    half = S5_STATE
    rows = u_ref.shape[0]
    u = u_ref[...]

    f1 = tab_ref[S5_TAB_F:S5_TAB_F + ts, :]
    f2 = tab_ref[S5_TAB_F + ts:S5_TAB_F + 2 * ts, :]
    pmat = jnp.concatenate(
        [(f1 * tab_ref[S5_TAB_B + hi:S5_TAB_B + hi + 1, :]
          + f2 * tab_ref[S5_TAB_B + h + hi:S5_TAB_B + h + hi + 1, :]).astype(BF16) for hi in range(h)], axis=0)
    e = _dot(u, pmat)

    row_idx = lax.broadcasted_iota(jnp.int32, (n_chunks, 2 * half), 0)
    hprev = []
    for b in range(rows // n_chunks):
        acc = e[b * n_chunks:(b + 1) * n_chunks, :]
        d = 1
        lvl = 0
        while d < n_chunks:
            sh = jnp.where(row_idx >= d, pltpu.roll(acc, d, 0), 0.0)
            acc = (acc + tab_ref[S5_TAB_SC + 2 * lvl:S5_TAB_SC + 2 * lvl + 1, :] * sh
                   + tab_ref[S5_TAB_SC + 2 * lvl + 1:S5_TAB_SC + 2 * lvl + 2, :] * pltpu.roll(sh, half, 1))
            d *= 2
            lvl += 1
        hprev.append(jnp.where(row_idx >= 1, pltpu.roll(acc, 1, 0), 0.0))
    hp = jnp.concatenate(hprev, axis=0).astype(BF16)

    s_idx = lax.broadcasted_iota(jnp.int32, (ts, LANES), 0)
    l_idx = lax.broadcasted_iota(jnp.int32, (ts, LANES), 1)
    keep = (l_idx & (ts - 1)) >= s_idx
    first = lax.broadcasted_iota(jnp.int32, (2 * half, LANES), 1) < ts
    slabs_per_tile = MXU_DIM // LANES
    kdim = h * ts
    for n in range(h * ts // MXU_DIM):
        for j in range(slabs_per_tile):
            m = n * slabs_per_tile + j
            cols = slice(j * LANES, (j + 1) * LANES)
            for hi in range(h):
                kb = jnp.broadcast_to(k_ref[hi * (h // 2) + m:hi * (h // 2) + m + 1, :], (ts, LANES))
                tz = pltpu.roll(kb, 0, 1, stride=1, stride_axis=0)
                w_ref[n, hi * ts:(hi + 1) * ts, cols] = jnp.where(keep, tz, 0.0).astype(BF16)
            c_re = tab_ref[S5_TAB_C:S5_TAB_C + 2 * half, :]
            c_im = tab_ref[S5_TAB_C + 2 * half:S5_TAB_C + 4 * half, :]
            k1 = jnp.where(first, jnp.broadcast_to(c_re[:, 2 * m:2 * m + 1], (2 * half, LANES)),
                           jnp.broadcast_to(c_re[:, 2 * m + 1:2 * m + 2], (2 * half, LANES)))
            k2 = jnp.where(first, jnp.broadcast_to(c_im[:, 2 * m:2 * m + 1], (2 * half, LANES)),
                           jnp.broadcast_to(c_im[:, 2 * m + 1:2 * m + 2], (2 * half, LANES)))
            a1 = tab_ref[S5_TAB_A:S5_TAB_A + 2 * half, :]
            a2 = tab_ref[S5_TAB_A + 2 * half:S5_TAB_A + 4 * half, :]
            w_ref[n, kdim:kdim + 2 * half, cols] = (k1 * a1 + k2 * a2).astype(BF16)
        y = _dot(u, w_ref[n, 0:kdim, :]) + _dot(hp, w_ref[n, kdim:kdim + 2 * half, :])
        y_ref[:, n * MXU_DIM:(n + 1) * MXU_DIM] = y.astype(y_ref.dtype)


def _s5_core(u2, taps2, tab, n_chunks):
    rows = u2.shape[0]
    g = taps2.shape[0]
    w = S5_GROUP * S5_CHUNK
    assert tab.shape[1:] == (S5_TAB_ROWS, LANES)
    tabs = [taps2, tab]
    blk = 2 * _nbytes((rows, w), BF16) + sum(_nbytes(t.shape[1:], F32) for t in tabs)
    wshape = (w // MXU_DIM, w + 2 * S5_STATE, MXU_DIM)

    def tab_spec(t):
        nd = t.ndim - 1
        return pl.BlockSpec((None,) + t.shape[1:], lambda i: (i,) + (0,) * nd)

    return pl.pallas_call(
        functools.partial(_s5_core_kernel, n_chunks=n_chunks), grid=(g,),
        in_specs=[pl.BlockSpec((rows, w), lambda i: (0, i))] + [tab_spec(t) for t in tabs],
        out_specs=pl.BlockSpec((rows, w), lambda i: (0, i)),
        out_shape=jax.ShapeDtypeStruct((rows, g * w), BF16),
        scratch_shapes=[pltpu.VMEM(wshape, BF16)],
        compiler_params=_params(("parallel",), blk, _nbytes(wshape, BF16)),
        name="s5_core")(u2, *tabs)


def _s5_prepare(lam_re, lam_im, log_dt, b_re, b_im, c_re, c_im, n_chunks):
    ts = S5_CHUNK
    assert 2 * ts == LANES and 2 * S5_STATE == LANES
    dt = jnp.exp(log_dt.astype(F32))[:, None]
    lr = jnp.minimum(lam_re.astype(F32), -1e-4)
    li = lam_im.astype(F32)
    lrd = lr * dt
    lid = li * dt
    mag = jnp.exp(lrd)
    ab_re = mag * jnp.cos(lid)
    ab_im = mag * jnp.sin(lid)
    den = lr * lr + li * li
    nr = ab_re - 1.0
    coef_re = ((nr * lr + ab_im * li) / den)[:, None, :]
    coef_im = ((ab_im * lr - nr * li) / den)[:, None, :]
    brt = b_re.astype(F32).transpose(0, 2, 1)
    bit = b_im.astype(F32).transpose(0, 2, 1)
    bbt_re = coef_re * brt - coef_im * bit
    bbt_im = coef_re * bit + coef_im * brt
    cr = c_re.astype(F32)
    ci = c_im.astype(F32)
    g = lr.shape[0]

    def power(expo, p_axis):
        shape = [g, 1, 1]
        shape[p_axis] = S5_STATE
        m = jnp.exp(lrd.reshape(shape) * expo)
        th = lid.reshape(shape) * expo
        return m * jnp.cos(th), m * jnp.sin(th)

    lag = jnp.arange(ts, dtype=F32)
    r_re = cr[:, None] * bbt_re[:, :, None] - ci[:, None] * bbt_im[:, :, None]
    r_im = cr[:, None] * bbt_im[:, :, None] + ci[:, None] * bbt_re[:, :, None]
    rcat = jnp.concatenate([r_re, -r_im], axis=-1).reshape(g, S5_GROUP * S5_GROUP, 2 * S5_STATE)
    pj_re, pj_im = power(lag[None, None, :], 1)
    apcat = jnp.concatenate([pj_re, pj_im], axis=1)
    fr, fi = power((ts - 1.0 - lag)[None, :, None], 2)
    wr, wi = power((lag + 1.0)[None, None, :], 1)
    a1 = jnp.concatenate([wr, -wi], axis=1)
    a2 = jnp.concatenate([-wi, -wr], axis=1)
    crt = jnp.pad(cr.transpose(0, 2, 1), ((0, 0), (0, 0), (0, LANES - S5_GROUP)))
    cit = jnp.pad(ci.transpose(0, 2, 1), ((0, 0), (0, 0), (0, LANES - S5_GROUP)))
    sc_rows = []
    d = 1
    while d < n_chunks:
        a_r, a_i = power(float(ts * d), 2)
        sc_rows += [jnp.concatenate([a_r, a_r], -1), jnp.concatenate([-a_i, a_i], -1)]
        d *= 2
    assert len(sc_rows) <= S5_SC_ROWS
    sc_rows.append(jnp.zeros((g, S5_SC_ROWS - len(sc_rows), 2 * S5_STATE), F32))
    tab = jnp.concatenate(
        [jnp.concatenate([fr, fr], -1), jnp.concatenate([-fi, fi], -1),
         jnp.concatenate([bbt_re, bbt_im], -1), jnp.concatenate([bbt_im, bbt_re], -1),
         jnp.concatenate([a1, a1], -1), jnp.concatenate([a2, a2], -1),
         *sc_rows,
         crt, crt, cit, cit], axis=1)
    return rcat, apcat, tab


def _ssd_kernel(xbc_ref, z_ref, dt_ref, shift_ref, tri_ref, exp_ref, cw_ref, cb_ref, dtb_ref, a_ref,
                dsk_ref, nw_ref, y_ref, xcat_ref, st_ref, yg_ref):
    t = CHUNK
    c = pl.program_id(0)

    @pl.when(c == 0)
    def _():
        xcat_ref[:, 0:t, :] = jnp.zeros((xcat_ref.shape[0], t, SSD_CONV_DIM), BF16)
        st_ref[...] = jnp.zeros_like(st_ref)

    @pl.when(c != 0)
    def _():
        xcat_ref[:, 0:t, :] = xcat_ref[:, t:2 * t, :]

    for b in range(xbc_ref.shape[0]):
        _ssd_chunk(xbc_ref.at[b], z_ref.at[b], dt_ref.at[b], shift_ref, tri_ref, exp_ref, cw_ref, cb_ref,
                   dtb_ref, a_ref, dsk_ref, nw_ref, y_ref.at[b], xcat_ref.at[b],
                   st_ref.at[pl.ds(b * SSD_GROUPS, SSD_GROUPS)], yg_ref.at[b])


def _ssd_chunk(xbc_ref, z_ref, dt_ref, shift_ref, tri_ref, exp_ref, cw_ref, cb_ref, dtb_ref, a_ref,
               dsk_ref, nw_ref, y_ref, xcat_ref, st_ref, yg_ref):
    t = CHUNK
    xcat_ref[t:2 * t, :] = xbc_ref[...]
    xsh = _dot(shift_ref[...], xcat_ref[...])
    acc = cb_ref[...] + cw_ref[SSD_CONV - 1:SSD_CONV, :] * xbc_ref[...].astype(F32)
    for k in range(SSD_CONV - 1):
        acc = acc + cw_ref[k:k + 1, :] * xsh[k * t:(k + 1) * t, :]
    xc = _silu(acc)
    gn = SSD_GROUPS * SSD_STATE
    xs = xc[:, :SSD_WIDTH]
    bm = xc[:, SSD_WIDTH:SSD_WIDTH + gn]
    cm = xc[:, SSD_WIDTH + gn:]

    xr = dt_ref[...] + dtb_ref[...]
    dt = jnp.maximum(xr, 0.0) + jnp.log(1.0 + jnp.exp(-jnp.abs(xr)))
    da = dt * a_ref[...]
    cs3 = _dot(tri_ref[...], jnp.concatenate(_split3(da), axis=1))
    a_cs = cs3[:, :LANES] + cs3[:, LANES:2 * LANES] + cs3[:, 2 * LANES:]
    expand = exp_ref[...]
    dt_e = _dot(jnp.concatenate(_split3(dt)[:2], axis=1), expand)
    acs_e = _dot(jnp.concatenate(_split3(a_cs)[:2], axis=1), expand)
    last_e = acs_e[t - 1:t, :]
    xdt = xs * dt_e
    xdt_b = xdt.astype(BF16)
    xd_b = (xdt * jnp.exp(last_e - acs_e)).astype(BF16)
    exp_acs = jnp.exp(acs_e)
    chunk_decay = jnp.exp(last_e)
    a_cs_t = a_cs.T
    r_i = lax.broadcasted_iota(jnp.int32, (t, t), 0)
    c_i = lax.broadcasted_iota(jnp.int32, (t, t), 1)
    causal = r_i >= c_i
    lane = lax.broadcasted_iota(jnp.int32, (t, LANES), 1)
    gw = SSD_REP * SSD_HEAD_DIM
    for g in range(SSD_GROUPS):
        bg = bm[:, g * SSD_STATE:(g + 1) * SSD_STATE]
        cg = cm[:, g * SSD_STATE:(g + 1) * SSD_STATE].astype(BF16)
        gmat = _dot_nt(cg, bg.astype(BF16))
        s_prev = st_ref[g]
        y_off = _dot(cg, s_prev.astype(BF16)) * exp_acs[:, g * gw:(g + 1) * gw]
        st_ref[g] = (s_prev * chunk_decay[:, g * gw:(g + 1) * gw]
                     + _dot(bg.T.astype(BF16), xd_b[:, g * gw:(g + 1) * gw]))
        for pr in range(SSD_REP // 2):
            ms = []
            for hh in range(2):
                hd = g * SSD_REP + 2 * pr + hh
                seg = a_cs[:, hd:hd + 1] - a_cs_t[hd:hd + 1, :]
                ms.append((gmat * jnp.exp(jnp.where(causal, seg, -jnp.inf))).astype(BF16))
            lo = g * gw + pr * LANES
            x2 = xdt_b[:, lo:lo + LANES]
            rhs = jnp.concatenate([jnp.where(lane < SSD_HEAD_DIM, x2, jnp.zeros_like(x2)),
                                   jnp.where(lane >= SSD_HEAD_DIM, x2, jnp.zeros_like(x2))], axis=0)
            y_d = _dot(jnp.concatenate(ms, axis=1), rhs)
            y = y_d + y_off[:, pr * LANES:(pr + 1) * LANES] + xs[:, lo:lo + LANES] * dsk_ref[:, lo:lo + LANES]
            yg_ref[:, lo:lo + LANES] = y * _silu(z_ref[:, lo:lo + LANES].astype(F32))
    for g in range(SSD_GROUPS):
        yg = yg_ref[:, g * gw:(g + 1) * gw]
        ms_ = jnp.mean(yg * yg, axis=-1, keepdims=True)
        y_ref[:, g * gw:(g + 1) * gw] = (yg * lax.rsqrt(ms_ + NORM_EPS)
                                         * nw_ref[:, g * gw:(g + 1) * gw]).astype(y_ref.dtype)


def _ssd(xbc, z, dt, conv_w, conv_b, dt_bias, a_neg, d_e, norm_w, batch, n_chunks):
    t = CHUNK
    n = xbc.shape[0]
    seq_len = n // batch
    row = lambda c: (0, c, 0)
    const = lambda c: (0, 0)
    xbc = xbc.reshape(batch, seq_len, SSD_CONV_DIM)
    z = z.reshape(batch, seq_len, z.shape[1])
    dt = dt.reshape(batch, seq_len, LANES)
    rr = jnp.arange((SSD_CONV - 1) * t)
    src = t + rr % t - (SSD_CONV - 1) + rr // t
    shift = (jnp.arange(2 * t)[None, :] == src[:, None]).astype(BF16)
    tri = (jnp.arange(t)[:, None] >= jnp.arange(t)[None, :]).astype(BF16)
    expand = (jnp.arange(SSD_WIDTH)[None, :] // SSD_HEAD_DIM
              == (jnp.arange(2 * LANES) % LANES)[:, None]).astype(BF16)
    blk = (batch * (_nbytes((t, SSD_CONV_DIM), BF16) + 2 * _nbytes((t, SSD_WIDTH), BF16)
                    + _nbytes((t, LANES), F32))
           + _nbytes(shift.shape, BF16) + _nbytes(tri.shape, BF16) + _nbytes(expand.shape, BF16)
           + 6 * _nbytes((SUBLANES, SSD_CONV_DIM), F32))
    st_shape = (batch * SSD_GROUPS, SSD_STATE, SSD_REP * SSD_HEAD_DIM)
    scratch = (_nbytes((batch, 2 * t, SSD_CONV_DIM), BF16) + _nbytes(st_shape, F32)
               + _nbytes((batch, t, SSD_WIDTH), F32))
    out = pl.pallas_call(
        _ssd_kernel, grid=(n_chunks,),
        in_specs=[pl.BlockSpec((batch, t, SSD_CONV_DIM), row),
                  pl.BlockSpec((batch, t, SSD_WIDTH), lambda c: (0, c, 1)),
                  pl.BlockSpec((batch, t, LANES), row),
                  pl.BlockSpec(shift.shape, const), pl.BlockSpec(tri.shape, const),
                  pl.BlockSpec(expand.shape, const),
                  pl.BlockSpec(conv_w.shape, const), pl.BlockSpec(conv_b.shape, const),
                  pl.BlockSpec(dt_bias.shape, const), pl.BlockSpec(a_neg.shape, const),
                  pl.BlockSpec(d_e.shape, const), pl.BlockSpec(norm_w.shape, const)],
        out_specs=pl.BlockSpec((batch, t, SSD_WIDTH), row),
        out_shape=jax.ShapeDtypeStruct((batch, seq_len, SSD_WIDTH), BF16),
        scratch_shapes=[pltpu.VMEM((batch, 2 * t, SSD_CONV_DIM), BF16),
                        pltpu.VMEM(st_shape, F32),
                        pltpu.VMEM((batch, t, SSD_WIDTH), F32)],
        compiler_params=_params(("arbitrary",), blk, scratch),
        name="ssd")(xbc, z, dt, shift, tri, expand, conv_w, conv_b, dt_bias, a_neg, d_e, norm_w)
    return out.reshape(n, SSD_WIDTH)


def _post0_kernel(ys_ref, u_ref, z_ref, yb_ref, x_ref, d_ref, gw_ref, gb_ref, wa_ref, wb_ref, o_ref):
    u = u_ref[...].astype(F32)
    v = ys_ref[...].astype(F32) + d_ref[...] * u
    y = 0.5 * v * (1.0 + jnp.tanh(math.sqrt(2.0 / math.pi) * (v + 0.044715 * (v * v * v))))
    gl = _dot(y.astype(BF16), gw_ref[...]) + gb_ref[...]
    ya = y * jax.nn.sigmoid(gl) * _silu(z_ref[...].astype(F32))
    out = _dot(ya.astype(BF16), wa_ref[...]) + _dot(yb_ref[...].astype(BF16), wb_ref[...])
    o_ref[...] = x_ref[...] + out


def _post0(ys, u, z, yb, x2d, s5_d, glu_w, glu_b, w_a, w_b):
    n, d = x2d.shape
    tm = min(TOKEN_TILE, n)
    row = lambda i: (i, 0)
    const = lambda i: (0, 0)
    blk = (3 * _nbytes((tm, d), BF16) + 3 * _nbytes((tm, d), F32) + 3 * _nbytes((d, d), BF16)
           + 2 * _nbytes((SUBLANES, d), F32))
    return pl.pallas_call(
        _post0_kernel, grid=(n // tm,),
        in_specs=[pl.BlockSpec((tm, d), row), pl.BlockSpec((tm, d), row), pl.BlockSpec((tm, d), row),
                  pl.BlockSpec((tm, d), row), pl.BlockSpec((tm, d), row),
                  pl.BlockSpec((1, d), const), pl.BlockSpec(glu_w.shape, const), pl.BlockSpec((1, d), const),
                  pl.BlockSpec(w_a.shape, const), pl.BlockSpec(w_b.shape, const)],
        out_specs=pl.BlockSpec((tm, d), row),
        out_shape=jax.ShapeDtypeStruct((n, d), F32),
        compiler_params=_params(("parallel",), blk), name="post0")(
            ys, u, z, yb, x2d, s5_d, glu_w, glu_b, w_a, w_b)


def _ret_kernel(q_ref, kt_ref, v_ref, g_ref, h_ref, dmat_ref, qd_ref, kdt_ref, cd_ref, gw_ref, gb_ref,
                wo_ref, fw_ref, o_ref, st_ref, y_ref, *, final_norm):
    c = pl.program_id(1)

    @pl.when(c == 0)
    def _():
        st_ref[...] = jnp.zeros_like(st_ref)

    for h in range(RET_HEADS):
        qh = q_ref[:, h * RET_QK_DIM:(h + 1) * RET_QK_DIM]
        kth = kt_ref[h * RET_QK_DIM:(h + 1) * RET_QK_DIM, :]
        vh = v_ref[:, h * RET_V_DIM:(h + 1) * RET_V_DIM]
        scores = _dot(qh, kth) * dmat_ref[h]
        inner = _dot(scores.astype(BF16), vh)
        s_prev = st_ref[h]
        qdec = (qh.astype(F32) * qd_ref[h]).astype(BF16)
        cross = _dot(qdec, s_prev.astype(BF16))
        kdec = (kth.astype(F32) * kdt_ref[h]).astype(BF16)
        st_ref[h] = s_prev * cd_ref[h] + _dot(kdec, vh)
        o = inner + cross
        mu = jnp.mean(o, axis=-1, keepdims=True)
        oc = o - mu
        var = jnp.mean(oc * oc, axis=-1, keepdims=True)
        sl = slice(h * RET_V_DIM, (h + 1) * RET_V_DIM)
        on = oc * lax.rsqrt(var + NORM_EPS) * gw_ref[:, sl] + gb_ref[:, sl]
        y_ref[:, sl] = (_silu(g_ref[:, sl].astype(F32)) * on).astype(y_ref.dtype)
    hn = h_ref[...] + _dot(y_ref[...], wo_ref[...])
    if final_norm:
        ms = jnp.mean(hn * hn, axis=-1, keepdims=True)
        hn = hn * lax.rsqrt(ms + NORM_EPS) * fw_ref[...]
    o_ref[...] = hn


def _retention(q, kt, v, g, h2d, dmat, qd, kdt, cd, gn_w, gn_b, w_out, final_w, batch, seq_len, final_norm):
    rt = dmat.shape[1]
    n_chunks = seq_len // rt
    n, d = h2d.shape
    row = lambda b, c: (b * n_chunks + c, 0)
    col = lambda b, c: (0, b * n_chunks + c)
    c2 = lambda b, c: (0, 0)
    c3 = lambda b, c: (0, 0, 0)
    blk = (2 * _nbytes((rt, D_MODEL), BF16) + 2 * _nbytes((rt, D_INNER), BF16) + 2 * _nbytes((rt, d), F32)
           + _nbytes(dmat.shape, F32) + 2 * _nbytes(qd.shape, F32) + _nbytes(cd.shape, F32)
           + _nbytes(w_out.shape, BF16) + 3 * _nbytes((SUBLANES, D_INNER), F32))
    scratch = _nbytes((RET_HEADS, RET_QK_DIM, RET_V_DIM), F32) + _nbytes((rt, D_INNER), BF16)
    return pl.pallas_call(
        functools.partial(_ret_kernel, final_norm=final_norm), grid=(batch, n_chunks),
        in_specs=[pl.BlockSpec((rt, D_MODEL), row), pl.BlockSpec((D_MODEL, rt), col),
                  pl.BlockSpec((rt, D_INNER), row), pl.BlockSpec((rt, D_INNER), row),
                  pl.BlockSpec((rt, d), row),
                  pl.BlockSpec(dmat.shape, c3), pl.BlockSpec(qd.shape, c3), pl.BlockSpec(kdt.shape, c3),
                  pl.BlockSpec(cd.shape, c3), pl.BlockSpec((1, D_INNER), c2), pl.BlockSpec((1, D_INNER), c2),
                  pl.BlockSpec(w_out.shape, c2), pl.BlockSpec((1, d), c2)],
        out_specs=pl.BlockSpec((rt, d), row),
        out_shape=jax.ShapeDtypeStruct((n, d), F32),
        scratch_shapes=[pltpu.VMEM((RET_HEADS, RET_QK_DIM, RET_V_DIM), F32),
                        pltpu.VMEM((rt, D_INNER), BF16)],
        compiler_params=_params(("arbitrary", "arbitrary"), blk, scratch),
        name="retention")(q, kt, v, g, h2d, dmat, qd, kdt, cd, gn_w, gn_b, w_out, final_w)


def _s5_ssd_layer(h2d, norm_w, batch, seq_len, w_in, lam_re, lam_im, log_dt, b_re, b_im, c_re, c_im,
                  s5_d, glu_w, glu_b, conv_w, conv_b, dt_bias, a_log, ssd_d, ssd_norm_w, w_out):
    n = batch * seq_len
    o1 = D_INNER
    o2 = o1 + S5_WIDTH
    o3 = o2 + SSD_CONV_DIM
    w_pad = jnp.pad(w_in, ((0, 0), (0, LANES - SSD_HEADS))).astype(BF16)
    z, u, xbc, dt = _norm_proj(h2d, norm_w, w_pad, [(0, o1), (o1, o2), (o2, o3), (o3, o3 + LANES)],
                               [BF16, BF16, BF16, F32])

    ts = S5_CHUNK
    nc_s = seq_len // ts
    rcat, apcat, tab = _s5_prepare(lam_re, lam_im, log_dt, b_re, b_im, c_re, c_im, nc_s)
    taps2 = _s5_taps(rcat, apcat).reshape(S5_GROUPS, S5_GROUP * S5_GROUP // 2, 2 * ts)
    u2 = u.reshape(batch * nc_s, ts, S5_WIDTH).transpose(0, 2, 1).reshape(batch * nc_s, S5_WIDTH * ts)
    y2 = _s5_core(u2, taps2, tab, nc_s)
    ys = y2.reshape(batch * nc_s, S5_WIDTH, ts).transpose(0, 2, 1).reshape(n, S5_WIDTH)

    pad_h = LANES - SSD_HEADS
    cw = jnp.pad(conv_w.astype(F32), ((0, SUBLANES - SSD_CONV), (0, 0)))
    cb = conv_b.astype(F32).reshape(1, SSD_CONV_DIM)
    dtb = jnp.pad(dt_bias.astype(F32), (0, pad_h)).reshape(1, LANES)
    a_neg = jnp.pad(-jnp.exp(a_log.astype(F32)), (0, pad_h)).reshape(1, LANES)
    d_e = jnp.repeat(ssd_d.astype(F32), SSD_HEAD_DIM).reshape(1, SSD_WIDTH)
    nw = ssd_norm_w.astype(F32).reshape(1, SSD_WIDTH)
    yb = _ssd(xbc, z, dt, cw, cb, dtb, a_neg, d_e, nw, batch, seq_len // CHUNK)

    return _post0(ys, u, z, yb, h2d, s5_d.astype(F32).reshape(1, S5_WIDTH), glu_w.astype(BF16),
                  glu_b.astype(F32).reshape(1, S5_WIDTH), w_out[:S5_WIDTH].astype(BF16),
                  w_out[S5_WIDTH:].astype(BF16))


def _retention_layer(h2d, norm_w, batch, seq_len, w_in, gn_w, gn_b, w_out, final_w, final_norm):
    qk_w = RET_HEADS * RET_QK_DIM
    half = RET_QK_DIM // 2

    w_bf = w_in.astype(BF16)
    w_kt = w_bf[:, qk_w:2 * qk_w].T
    angle = 1.0 / (ROPE_BASE ** jnp.linspace(0.0, 1.0, half, dtype=F32))
    q, kt, v, g = _ret_proj(h2d, norm_w, angle, w_bf, w_kt, seq_len)

    rt = min(RET_CHUNK, seq_len)
    log_gamma = jnp.log(1.0 - 2.0 ** (-5.0 - jnp.arange(RET_HEADS, dtype=F32)))
    p = jnp.arange(rt, dtype=F32)
    rel = p[:, None] - p[None, :]
    dmat = jnp.where(rel >= 0, jnp.exp(log_gamma[:, None, None] * jnp.maximum(rel, 0.0)), 0.0)
    qd = jnp.broadcast_to(jnp.exp(log_gamma[:, None] * (p[None, :] + 1.0))[..., None],
                          (RET_HEADS, rt, RET_QK_DIM))
    kdt = jnp.broadcast_to(jnp.exp(log_gamma[:, None] * (rt - 1.0 - p)[None, :])[:, None, :],
                           (RET_HEADS, RET_QK_DIM, rt))
    cd = jnp.broadcast_to(jnp.exp(log_gamma * rt)[:, None, None], (RET_HEADS, 1, RET_V_DIM))
    return _retention(q, kt, v, g, h2d, dmat, qd, kdt, cd, gn_w.astype(F32).reshape(1, D_INNER),
                      gn_b.astype(F32).reshape(1, D_INNER), w_out.astype(BF16),
                      final_w.astype(F32).reshape(1, D_MODEL), batch, seq_len, final_norm)


def kernel(x, layer_norm_w, ab_w_in, s5_lam_re, s5_lam_im, s5_log_dt, s5_b_re, s5_b_im, s5_c_re, s5_c_im,
           s5_d, s5_glu_w, s5_glu_b, ssd_conv_w, ssd_conv_b, ssd_dt_bias, ssd_a_log, ssd_d, ssd_norm_w,
           ab_w_out, ret_w_in, ret_gn_w, ret_gn_b, ret_w_out, final_norm_w):
    batch, seq_len, d = x.shape
    depth = layer_norm_w.shape[0]
    assert depth % 2 == 0 and seq_len % CHUNK == 0 and d == D_MODEL
    h = x.astype(F32).reshape(batch * seq_len, d)
    for i in range(depth):
        j = i // 2
        if i % 2 == 0:
            h = _s5_ssd_layer(h, layer_norm_w[i], batch, seq_len, ab_w_in[j], s5_lam_re[j], s5_lam_im[j],
                              s5_log_dt[j], s5_b_re[j], s5_b_im[j], s5_c_re[j], s5_c_im[j], s5_d[j],
                              s5_glu_w[j], s5_glu_b[j], ssd_conv_w[j], ssd_conv_b[j], ssd_dt_bias[j],
                              ssd_a_log[j], ssd_d[j], ssd_norm_w[j], ab_w_out[j])
        else:
            h = _retention_layer(h, layer_norm_w[i], batch, seq_len, ret_w_in[j], ret_gn_w[j], ret_gn_b[j],
                                 ret_w_out[j], final_norm_w, final_norm=(i == depth - 1))
    return h.reshape(batch, seq_len, d).astype(x.dtype)
```

```python
import functools
import math

import jax
import jax.numpy as jnp
from jax import lax
from jax.experimental import pallas as pl
from jax.experimental.pallas import tpu as pltpu

F32 = jnp.float32
BF16 = jnp.bfloat16

D_MODEL = 1024
D_INNER = 2048
CHUNK = 128
NORM_EPS = 1e-6
S5_WIDTH = 1024
S5_GROUP = 16
S5_GROUPS = 64
S5_STATE = 64
SSD_WIDTH = 1024
SSD_HEAD_DIM = 64
SSD_HEADS = 16
SSD_GROUPS = 2
SSD_REP = 8
SSD_STATE = 128
SSD_CONV = 4
SSD_CONV_DIM = 1536
RET_HEADS = 4
RET_QK_DIM = 256
RET_V_DIM = 512
ROPE_BASE = 10000.0

LANES = 128
SUBLANES = 8
MXU_DIM = 256
VMEM_CAP = 60 * 1024 * 1024
TOKEN_TILE = 512
S5_CHUNK = 64
RET_CHUNK = 256


def _params(semantics, block_bytes, scratch_bytes=0):
    need = 2 * block_bytes + scratch_bytes + 16 * 1024 * 1024
    return pltpu.CompilerParams(dimension_semantics=semantics,
                                vmem_limit_bytes=int(min(need, VMEM_CAP)))


def _nbytes(shape, dtype):
    return math.prod(shape) * jnp.dtype(dtype).itemsize


def _sigmoid(x):
    return 0.5 + 0.5 * jnp.tanh(0.5 * x)


def _silu(x):
    return x * _sigmoid(x)


def _split3(x):
    hi = x.astype(BF16)
    r1 = x - hi.astype(F32)
    mid = r1.astype(BF16)
    lo = (r1 - mid.astype(F32)).astype(BF16)
    return hi, mid, lo


def _dot(a, b):
    return jnp.dot(a, b, preferred_element_type=F32)


def _dot_nt(a, b):
    return lax.dot_general(a, b, (((1,), (1,)), ((), ())), preferred_element_type=F32)


def _rmsnorm_bf16(x, w):
    ms = jnp.mean(x * x, axis=-1, keepdims=True)
    return (x * lax.rsqrt(ms + NORM_EPS) * w).astype(BF16)


def _norm_proj_kernel(*refs, n_out):
    x_ref, nw_ref = refs[0], refs[1]
    w_refs = refs[2:2 + n_out]
    o_refs = refs[2 + n_out:2 + 2 * n_out]
    xn = _rmsnorm_bf16(x_ref[...], nw_ref[...])
    for w_ref, o_ref in zip(w_refs, o_refs):
        o_ref[...] = _dot(xn, w_ref[...]).astype(o_ref.dtype)


def _norm_proj(x2d, norm_w, weights, out_dtypes):
    n, d = x2d.shape
    tm = min(TOKEN_TILE, n)
    row = lambda i: (i, 0)
    const = lambda i: (0, 0)
    in_specs = [pl.BlockSpec((tm, d), row), pl.BlockSpec((1, d), const)]
    blk = _nbytes((tm, d), F32)
    out_specs, out_shape = [], []
    for w, dt in zip(weights, out_dtypes):
        in_specs.append(pl.BlockSpec(w.shape, const))
        out_specs.append(pl.BlockSpec((tm, w.shape[1]), row))
        out_shape.append(jax.ShapeDtypeStruct((n, w.shape[1]), dt))
        blk += _nbytes(w.shape, w.dtype) + _nbytes((tm, w.shape[1]), dt)
    return pl.pallas_call(
        functools.partial(_norm_proj_kernel, n_out=len(weights)), grid=(n // tm,),
        in_specs=in_specs, out_specs=out_specs, out_shape=out_shape,
        compiler_params=_params(("parallel",), blk), name="norm_proj")(
            x2d, norm_w.reshape(1, d), *weights)


def _ret_proj_kernel(x_ref, nw_ref, cr_ref, sr_ref, cb_ref, sb_ref, crt_ref, srt_ref, cbt_ref, sbt_ref,
                     wq_ref, wkt_ref, wv_ref, wg_ref, q_ref, kt_ref, v_ref, g_ref):
    xn = _rmsnorm_bf16(x_ref[...], nw_ref[...])
    half = RET_QK_DIM // 2
    accq = _dot(xn, wq_ref[...])
    cos = cb_ref[...] * cr_ref[...] - sb_ref[...] * sr_ref[...]
    sin = sb_ref[...] * cr_ref[...] + cb_ref[...] * sr_ref[...]
    for h in range(RET_HEADS):
        x1 = accq[:, h * RET_QK_DIM:h * RET_QK_DIM + half]
        x2 = accq[:, h * RET_QK_DIM + half:(h + 1) * RET_QK_DIM]
        q_ref[:, h * RET_QK_DIM:h * RET_QK_DIM + half] = (x1 * cos - x2 * sin).astype(q_ref.dtype)
        q_ref[:, h * RET_QK_DIM + half:(h + 1) * RET_QK_DIM] = (x2 * cos + x1 * sin).astype(q_ref.dtype)
    acck = _dot_nt(wkt_ref[...], xn)
    reps = crt_ref.shape[1] // LANES
    cbt = jnp.concatenate([cbt_ref[...]] * reps, axis=1)
    sbt = jnp.concatenate([sbt_ref[...]] * reps, axis=1)
    cost = cbt * crt_ref[...] - sbt * srt_ref[...]
    sint = sbt * crt_ref[...] + cbt * srt_ref[...]
    scale = RET_QK_DIM ** -0.5
    for h in range(RET_HEADS):
        x1 = acck[h * RET_QK_DIM:h * RET_QK_DIM + half, :]
        x2 = acck[h * RET_QK_DIM + half:(h + 1) * RET_QK_DIM, :]
        kt_ref[h * RET_QK_DIM:h * RET_QK_DIM + half, :] = ((x1 * cost - x2 * sint) * scale).astype(kt_ref.dtype)
        kt_ref[h * RET_QK_DIM + half:(h + 1) * RET_QK_DIM, :] = ((x2 * cost + x1 * sint) * scale).astype(kt_ref.dtype)
    v_ref[...] = _dot(xn, wv_ref[...]).astype(v_ref.dtype)
    g_ref[...] = _dot(xn, wg_ref[...]).astype(g_ref.dtype)


def _ret_proj(x2d, norm_w, angle, w_q, w_kt, w_v, w_g, seq_len):
    n, d = x2d.shape
    tm = min(TOKEN_TILE, seq_len)
    tiles_per_seq = seq_len // tm
    half = angle.shape[0]
    qk_w = w_q.shape[1]
    row = lambda i: (i, 0)
    col = lambda i: (0, i)
    const = lambda i: (0, 0)
    th_r = jnp.arange(tm, dtype=F32)[:, None] * angle[None, :]
    th_b = (jnp.arange(tiles_per_seq, dtype=F32) * tm)[:, None] * angle[None, :]
    cos_r, sin_r = jnp.cos(th_r), jnp.sin(th_r)
    cos_b, sin_b = jnp.cos(th_b)[:, None, :], jnp.sin(th_b)[:, None, :]
    cos_bt = jnp.broadcast_to(jnp.cos(th_b)[:, :, None], (tiles_per_seq, half, LANES))
    sin_bt = jnp.broadcast_to(jnp.sin(th_b)[:, :, None], (tiles_per_seq, half, LANES))
    base = lambda i: (i % tiles_per_seq, 0, 0)
    blk = (_nbytes((tm, d), F32) + 4 * _nbytes((tm, half), F32) + 2 * _nbytes((half, LANES), F32)
           + 2 * _nbytes(w_q.shape, BF16)
           + 2 * _nbytes(w_v.shape, BF16) + 2 * _nbytes((tm, qk_w), BF16) + 2 * _nbytes((tm, D_INNER), BF16))
    return pl.pallas_call(
        _ret_proj_kernel, grid=(n // tm,),
        in_specs=[pl.BlockSpec((tm, d), row), pl.BlockSpec((1, d), const),
                  pl.BlockSpec((tm, half), const), pl.BlockSpec((tm, half), const),
                  pl.BlockSpec((None, 1, half), base), pl.BlockSpec((None, 1, half), base),
                  pl.BlockSpec((half, tm), const), pl.BlockSpec((half, tm), const),
                  pl.BlockSpec((None, half, LANES), base), pl.BlockSpec((None, half, LANES), base),
                  pl.BlockSpec(w_q.shape, const), pl.BlockSpec(w_kt.shape, const),
                  pl.BlockSpec(w_v.shape, const), pl.BlockSpec(w_g.shape, const)],
        out_specs=[pl.BlockSpec((tm, qk_w), row), pl.BlockSpec((qk_w, tm), col),
                   pl.BlockSpec((tm, D_INNER), row), pl.BlockSpec((tm, D_INNER), row)],
        out_shape=[jax.ShapeDtypeStruct((n, qk_w), BF16), jax.ShapeDtypeStruct((qk_w, n), BF16),
                   jax.ShapeDtypeStruct((n, D_INNER), BF16), jax.ShapeDtypeStruct((n, D_INNER), BF16)],
        compiler_params=_params(("parallel",), blk), name="ret_proj")(
            x2d, norm_w.reshape(1, d), cos_r, sin_r, cos_b, sin_b, cos_r.T, sin_r.T, cos_bt, sin_bt,
            w_q, w_kt, w_v, w_g)


def _s5_taps_kernel(r_ref, a_ref, k_ref):
    k_ref[...] = jnp.dot(r_ref[...], a_ref[...], preferred_element_type=F32,
                         precision=lax.Precision.HIGHEST)


def _s5_taps(rcat, apcat):
    g, m, kk = rcat.shape
    t = apcat.shape[2]
    return pl.pallas_call(
        _s5_taps_kernel, grid=(g,),
        in_specs=[pl.BlockSpec((None, m, kk), lambda i: (i, 0, 0)),
                  pl.BlockSpec((None, kk, t), lambda i: (i, 0, 0))],
        out_specs=pl.BlockSpec((None, m, t), lambda i: (i, 0, 0)),
        out_shape=jax.ShapeDtypeStruct((g, m, t), F32),
        compiler_params=_params(("parallel",), _nbytes((m, kk + t), F32) + _nbytes((kk, t), F32)),
        name="s5_taps")(rcat, apcat)


def _s5_core_kernel(u_ref, k_ref, f_ref, bt_ref, a_ref, ct_ref, sc_ref, y_ref, w_ref, *, n_chunks):
    ts = S5_CHUNK
    h = S5_GROUP
    half = S5_STATE
    rows = u_ref.shape[0]
    u = u_ref[...]

    f1 = f_ref[0]
    f2 = f_ref[1]
    pmat = jnp.concatenate(
        [(f1 * bt_ref[0, hi:hi + 1, :] + f2 * bt_ref[1, hi:hi + 1, :]).astype(BF16) for hi in range(h)], axis=0)
    e = _dot(u, pmat)

    row_idx = lax.broadcasted_iota(jnp.int32, (n_chunks, 2 * half), 0)
    hprev = []
    for b in range(rows // n_chunks):
        acc = e[b * n_chunks:(b + 1) * n_chunks, :]
        acc_sw = pltpu.roll(acc, half, 1)
        d = 1
        lvl = 0
        while d < n_chunks:
            c1 = sc_ref[2 * lvl:2 * lvl + 1, :]
            c2 = sc_ref[2 * lvl + 1:2 * lvl + 2, :]
            sh = jnp.where(row_idx >= d, pltpu.roll(acc, d, 0), 0.0)
            sh_sw = jnp.where(row_idx >= d, pltpu.roll(acc_sw, d, 0), 0.0)
            acc = acc + c1 * sh + c2 * sh_sw
            acc_sw = acc_sw + c1 * sh_sw - c2 * sh
            d *= 2
            lvl += 1
        hprev.append(jnp.where(row_idx >= 1, pltpu.roll(acc, 1, 0), 0.0))
    hp = jnp.concatenate(hprev, axis=0).astype(BF16)

    s_idx = lax.broadcasted_iota(jnp.int32, (ts, LANES), 0)
    l_idx = lax.broadcasted_iota(jnp.int32, (ts, LANES), 1)
    keep = (l_idx & (ts - 1)) >= s_idx
    first = lax.broadcasted_iota(jnp.int32, (2 * half, LANES), 1) < ts
    slabs_per_tile = MXU_DIM // LANES
    kdim = h * ts
    for n in range(h * ts // MXU_DIM):
        for j in range(slabs_per_tile):
            m = n * slabs_per_tile + j
            cols = slice(j * LANES, (j + 1) * LANES)
            for hi in range(h):
                kb = jnp.broadcast_to(k_ref[hi * (h // 2) + m:hi * (h // 2) + m + 1, :], (ts, LANES))
                tz = pltpu.roll(kb, 0, 1, stride=1, stride_axis=0)
                w_ref[n, hi * ts:(hi + 1) * ts, cols] = jnp.where(keep, tz, 0.0).astype(BF16)
            k1 = jnp.where(first, jnp.broadcast_to(ct_ref[0, :, 2 * m:2 * m + 1], (2 * half, LANES)),
                           jnp.broadcast_to(ct_ref[0, :, 2 * m + 1:2 * m + 2], (2 * half, LANES)))
            k2 = jnp.where(first, jnp.broadcast_to(ct_ref[1, :, 2 * m:2 * m + 1], (2 * half, LANES)),
                           jnp.broadcast_to(ct_ref[1, :, 2 * m + 1:2 * m + 2], (2 * half, LANES)))
            w_ref[n, kdim:kdim + 2 * half, cols] = (k1 * a_ref[0] + k2 * a_ref[1]).astype(BF16)
        y = _dot(u, w_ref[n, 0:kdim, :]) + _dot(hp, w_ref[n, kdim:kdim + 2 * half, :])
        y_ref[:, n * MXU_DIM:(n + 1) * MXU_DIM] = y.astype(y_ref.dtype)


def _s5_core(u2, taps2, ftab, bttab, atab, cttab, sc, n_chunks):
    rows = u2.shape[0]
    g = taps2.shape[0]
    w = S5_GROUP * S5_CHUNK
    tabs = [taps2, ftab, bttab, atab, cttab, sc]
    blk = 2 * _nbytes((rows, w), BF16) + sum(_nbytes(t.shape[1:], F32) for t in tabs)
    wshape = (w // MXU_DIM, w + 2 * S5_STATE, MXU_DIM)

    def tab_spec(t):
        nd = t.ndim - 1
        return pl.BlockSpec((None,) + t.shape[1:], lambda i: (i,) + (0,) * nd)

    return pl.pallas_call(
        functools.partial(_s5_core_kernel, n_chunks=n_chunks), grid=(g,),
        in_specs=[pl.BlockSpec((rows, w), lambda i: (0, i))] + [tab_spec(t) for t in tabs],
        out_specs=pl.BlockSpec((rows, w), lambda i: (0, i)),
        out_shape=jax.ShapeDtypeStruct((rows, g * w), BF16),
        scratch_shapes=[pltpu.VMEM(wshape, BF16)],
        compiler_params=_params(("parallel",), blk, _nbytes(wshape, BF16)),
        name="s5_core")(u2, *tabs)


def _s5_prepare(lam_re, lam_im, log_dt, b_re, b_im, c_re, c_im, n_chunks):
    ts = S5_CHUNK
    assert 2 * ts == LANES and 2 * S5_STATE == LANES
    dt = jnp.exp(log_dt.astype(F32))[:, None]
    lr = jnp.minimum(lam_re.astype(F32), -1e-4)
    li = lam_im.astype(F32)
    mag = jnp.exp(lr * dt)
    ab_re = mag * jnp.cos(li * dt)
    ab_im = mag * jnp.sin(li * dt)
    den = lr * lr + li * li
    nr = ab_re - 1.0
    coef_re = (nr * lr + ab_im * li) / den
    coef_im = (ab_im * lr - nr * li) / den
    br = b_re.astype(F32)
    bi = b_im.astype(F32)
    bb_re = coef_re[..., None] * br - coef_im[..., None] * bi
    bb_im = coef_re[..., None] * bi + coef_im[..., None] * br
    cr = c_re.astype(F32)
    ci = c_im.astype(F32)
    g = lr.shape[0]
    j = jnp.arange(ts + 1, dtype=F32)
    pmag = jnp.exp((lr * dt)[..., None] * j)
    ang = (li * dt)[..., None] * j
    pw_re = pmag * jnp.cos(ang)
    pw_im = pmag * jnp.sin(ang)
    bbt_re = bb_re.transpose(0, 2, 1)
    bbt_im = bb_im.transpose(0, 2, 1)
    r_re = cr[:, None] * bbt_re[:, :, None] - ci[:, None] * bbt_im[:, :, None]
    r_im = cr[:, None] * bbt_im[:, :, None] + ci[:, None] * bbt_re[:, :, None]
    rcat = jnp.concatenate([r_re, -r_im], axis=-1).reshape(g, S5_GROUP * S5_GROUP, 2 * S5_STATE)
    apcat = jnp.concatenate([pw_re[..., :ts], pw_im[..., :ts]], axis=1)
    fr = pw_re[..., ts - 1::-1].transpose(0, 2, 1)
    fi = pw_im[..., ts - 1::-1].transpose(0, 2, 1)
    ftab = jnp.stack([jnp.concatenate([fr, fr], -1), jnp.concatenate([-fi, fi], -1)], axis=1)
    bttab = jnp.stack([jnp.concatenate([bbt_re, bbt_im], -1), jnp.concatenate([bbt_im, bbt_re], -1)], axis=1)
    wr = pw_re[..., 1:]
    wi = pw_im[..., 1:]
    a1 = jnp.concatenate([wr, -wi], axis=1)
    a2 = jnp.concatenate([-wi, -wr], axis=1)
    atab = jnp.stack([jnp.concatenate([a1, a1], -1), jnp.concatenate([a2, a2], -1)], axis=1)
    crt = cr.transpose(0, 2, 1)
    cit = ci.transpose(0, 2, 1)
    cttab = jnp.stack([jnp.concatenate([crt, crt], 1), jnp.concatenate([cit, cit], 1)], axis=1)
    rows = []
    d = 1
    while d < n_chunks:
        ex = float(ts * d)
        m = jnp.exp(lr * dt * ex)
        a_r = m * jnp.cos(li * dt * ex)
        a_i = m * jnp.sin(li * dt * ex)
        rows.append(jnp.concatenate([a_r, a_r], axis=-1))
        rows.append(jnp.concatenate([-a_i, a_i], axis=-1))
        d *= 2
    while len(rows) % SUBLANES or not rows:
        rows.append(jnp.zeros((g, 2 * S5_STATE), F32))
    sc = jnp.stack(rows, axis=1)
    return rcat, apcat, ftab, bttab, atab, cttab, sc


def _ssd_kernel(xbc_ref, z_ref, dt_ref, shift_ref, tri_ref, exp_ref, cw_ref, cb_ref, dtb_ref, a_ref,
                dsk_ref, nw_ref, y_ref, xcat_ref, st_ref, yg_ref):
    t = CHUNK
    c = pl.program_id(0)

    @pl.when(c == 0)
    def _():
        xcat_ref[:, 0:t, :] = jnp.zeros((xcat_ref.shape[0], t, SSD_CONV_DIM), BF16)
        st_ref[...] = jnp.zeros_like(st_ref)

    @pl.when(c != 0)
    def _():
        xcat_ref[:, 0:t, :] = xcat_ref[:, t:2 * t, :]

    for b in range(xbc_ref.shape[0]):
        _ssd_chunk(xbc_ref.at[b], z_ref.at[b], dt_ref.at[b], shift_ref, tri_ref, exp_ref, cw_ref, cb_ref,
                   dtb_ref, a_ref, dsk_ref, nw_ref, y_ref.at[b], xcat_ref.at[b],
                   st_ref.at[pl.ds(b * SSD_GROUPS, SSD_GROUPS)], yg_ref.at[b])


def _ssd_chunk(xbc_ref, z_ref, dt_ref, shift_ref, tri_ref, exp_ref, cw_ref, cb_ref, dtb_ref, a_ref,
               dsk_ref, nw_ref, y_ref, xcat_ref, st_ref, yg_ref):
    t = CHUNK
    xcat_ref[t:2 * t, :] = xbc_ref[...]
    xsh = _dot(shift_ref[...], xcat_ref[...])
    acc = cb_ref[...] + cw_ref[SSD_CONV - 1:SSD_CONV, :] * xbc_ref[...].astype(F32)
    for k in range(SSD_CONV - 1):
        acc = acc + cw_ref[k:k + 1, :] * xsh[k * t:(k + 1) * t, :]
    xc = _silu(acc)
    gn = SSD_GROUPS * SSD_STATE
    xs = xc[:, :SSD_WIDTH]
    bm = xc[:, SSD_WIDTH:SSD_WIDTH + gn]
    cm = xc[:, SSD_WIDTH + gn:]

    xr = dt_ref[...] + dtb_ref[...]
    dt = jnp.maximum(xr, 0.0) + jnp.log(1.0 + jnp.exp(-jnp.abs(xr)))
    da = dt * a_ref[...]
    cs3 = _dot(tri_ref[...], jnp.concatenate(_split3(da), axis=1))
    a_cs = cs3[:, :LANES] + cs3[:, LANES:2 * LANES] + cs3[:, 2 * LANES:]
    expand = exp_ref[...]
    dt_e = _dot(jnp.concatenate(_split3(dt)[:2], axis=1), expand)
    acs_e = _dot(jnp.concatenate(_split3(a_cs)[:2], axis=1), expand)
    last_e = acs_e[t - 1:t, :]
    xdt = xs * dt_e
    xdt_b = xdt.astype(BF16)
    xd_b = (xdt * jnp.exp(last_e - acs_e)).astype(BF16)
    exp_acs = jnp.exp(acs_e)
    chunk_decay = jnp.exp(last_e)
    a_cs_t = a_cs.T
    r_i = lax.broadcasted_iota(jnp.int32, (t, t), 0)
    c_i = lax.broadcasted_iota(jnp.int32, (t, t), 1)
    causal = r_i >= c_i
    lane = lax.broadcasted_iota(jnp.int32, (t, LANES), 1)
    gw = SSD_REP * SSD_HEAD_DIM
    for g in range(SSD_GROUPS):
        bg = bm[:, g * SSD_STATE:(g + 1) * SSD_STATE]
        cg = cm[:, g * SSD_STATE:(g + 1) * SSD_STATE].astype(BF16)
        gmat = _dot_nt(cg, bg.astype(BF16))
        s_prev = st_ref[g]
        y_off = _dot(cg, s_prev.astype(BF16)) * exp_acs[:, g * gw:(g + 1) * gw]
        st_ref[g] = (s_prev * chunk_decay[:, g * gw:(g + 1) * gw]
                     + _dot(bg.T.astype(BF16), xd_b[:, g * gw:(g + 1) * gw]))
        for pr in range(SSD_REP // 2):
            ms = []
            for hh in range(2):
                hd = g * SSD_REP + 2 * pr + hh
                seg = a_cs[:, hd:hd + 1] - a_cs_t[hd:hd + 1, :]
                ms.append((gmat * jnp.exp(jnp.where(causal, seg, -jnp.inf))).astype(BF16))
            lo = g * gw + pr * LANES
            x2 = xdt_b[:, lo:lo + LANES]
            rhs = jnp.concatenate([jnp.where(lane < SSD_HEAD_DIM, x2, jnp.zeros_like(x2)),
                                   jnp.where(lane >= SSD_HEAD_DIM, x2, jnp.zeros_like(x2))], axis=0)
            y_d = _dot(jnp.concatenate(ms, axis=1), rhs)
            y = y_d + y_off[:, pr * LANES:(pr + 1) * LANES] + xs[:, lo:lo + LANES] * dsk_ref[:, lo:lo + LANES]
            yg_ref[:, lo:lo + LANES] = y * _silu(z_ref[:, lo:lo + LANES].astype(F32))
    for g in range(SSD_GROUPS):
        yg = yg_ref[:, g * gw:(g + 1) * gw]
        ms_ = jnp.mean(yg * yg, axis=-1, keepdims=True)
        y_ref[:, g * gw:(g + 1) * gw] = (yg * lax.rsqrt(ms_ + NORM_EPS)
                                         * nw_ref[:, g * gw:(g + 1) * gw]).astype(y_ref.dtype)


def _ssd(xbc, z, dt, conv_w, conv_b, dt_bias, a_neg, d_e, norm_w, batch, n_chunks):
    t = CHUNK
    n = xbc.shape[0]
    seq_len = n // batch
    row = lambda c: (0, c, 0)
    const = lambda c: (0, 0)
    xbc = xbc.reshape(batch, seq_len, SSD_CONV_DIM)
    z = z.reshape(batch, seq_len, z.shape[1])
    dt = dt.reshape(batch, seq_len, LANES)
    rr = jnp.arange((SSD_CONV - 1) * t)
    src = t + rr % t - (SSD_CONV - 1) + rr // t
    shift = (jnp.arange(2 * t)[None, :] == src[:, None]).astype(BF16)
    tri = (jnp.arange(t)[:, None] >= jnp.arange(t)[None, :]).astype(BF16)
    expand = (jnp.arange(SSD_WIDTH)[None, :] // SSD_HEAD_DIM
              == (jnp.arange(2 * LANES) % LANES)[:, None]).astype(BF16)
    blk = (batch * (_nbytes((t, SSD_CONV_DIM), BF16) + 2 * _nbytes((t, SSD_WIDTH), BF16)
                    + _nbytes((t, LANES), F32))
           + _nbytes(shift.shape, BF16) + _nbytes(tri.shape, BF16) + _nbytes(expand.shape, BF16)
           + 6 * _nbytes((SUBLANES, SSD_CONV_DIM), F32))
    st_shape = (batch * SSD_GROUPS, SSD_STATE, SSD_REP * SSD_HEAD_DIM)
    scratch = (_nbytes((batch, 2 * t, SSD_CONV_DIM), BF16) + _nbytes(st_shape, F32)
               + _nbytes((batch, t, SSD_WIDTH), F32))
    out = pl.pallas_call(
        _ssd_kernel, grid=(n_chunks,),
        in_specs=[pl.BlockSpec((batch, t, SSD_CONV_DIM), row),
                  pl.BlockSpec((batch, t, SSD_WIDTH), lambda c: (0, c, 1)),
                  pl.BlockSpec((batch, t, LANES), row),
                  pl.BlockSpec(shift.shape, const), pl.BlockSpec(tri.shape, const),
                  pl.BlockSpec(expand.shape, const),
                  pl.BlockSpec(conv_w.shape, const), pl.BlockSpec(conv_b.shape, const),
                  pl.BlockSpec(dt_bias.shape, const), pl.BlockSpec(a_neg.shape, const),
                  pl.BlockSpec(d_e.shape, const), pl.BlockSpec(norm_w.shape, const)],
        out_specs=pl.BlockSpec((batch, t, SSD_WIDTH), row),
        out_shape=jax.ShapeDtypeStruct((batch, seq_len, SSD_WIDTH), BF16),
        scratch_shapes=[pltpu.VMEM((batch, 2 * t, SSD_CONV_DIM), BF16),
                        pltpu.VMEM(st_shape, F32),
                        pltpu.VMEM((batch, t, SSD_WIDTH), F32)],
        compiler_params=_params(("arbitrary",), blk, scratch),
        name="ssd")(xbc, z, dt, shift, tri, expand, conv_w, conv_b, dt_bias, a_neg, d_e, norm_w)
    return out.reshape(n, SSD_WIDTH)


def _post0_kernel(ys_ref, u_ref, z_ref, yb_ref, x_ref, d_ref, gw_ref, gb_ref, wa_ref, wb_ref, o_ref):
    u = u_ref[...].astype(F32)
    v = ys_ref[...].astype(F32) + d_ref[...] * u
    y = 0.5 * v * (1.0 + jnp.tanh(math.sqrt(2.0 / math.pi) * (v + 0.044715 * (v * v * v))))
    gl = _dot(y.astype(BF16), gw_ref[...]) + gb_ref[...]
    ya = y * _sigmoid(gl) * _silu(z_ref[...].astype(F32))
    out = _dot(ya.astype(BF16), wa_ref[...]) + _dot(yb_ref[...].astype(BF16), wb_ref[...])
    o_ref[...] = x_ref[...] + out


def _post0(ys, u, z, yb, x2d, s5_d, glu_w, glu_b, w_a, w_b):
    n, d = x2d.shape
    tm = min(TOKEN_TILE, n)
    row = lambda i: (i, 0)
    const = lambda i: (0, 0)
    blk = (3 * _nbytes((tm, d), BF16) + 3 * _nbytes((tm, d), F32) + 3 * _nbytes((d, d), BF16)
           + 2 * _nbytes((SUBLANES, d), F32))
    return pl.pallas_call(
        _post0_kernel, grid=(n // tm,),
        in_specs=[pl.BlockSpec((tm, d), row), pl.BlockSpec((tm, d), row), pl.BlockSpec((tm, d), row),
                  pl.BlockSpec((tm, d), row), pl.BlockSpec((tm, d), row),
                  pl.BlockSpec((1, d), const), pl.BlockSpec(glu_w.shape, const), pl.BlockSpec((1, d), const),
                  pl.BlockSpec(w_a.shape, const), pl.BlockSpec(w_b.shape, const)],
        out_specs=pl.BlockSpec((tm, d), row),
        out_shape=jax.ShapeDtypeStruct((n, d), F32),
        compiler_params=_params(("parallel",), blk), name="post0")(
            ys, u, z, yb, x2d, s5_d, glu_w, glu_b, w_a, w_b)


def _ret_kernel(q_ref, kt_ref, v_ref, g_ref, h_ref, dmat_ref, qd_ref, kdt_ref, cd_ref, gw_ref, gb_ref,
                wo_ref, fw_ref, o_ref, st_ref, y_ref, *, final_norm):
    c = pl.program_id(1)

    @pl.when(c == 0)
    def _():
        st_ref[...] = jnp.zeros_like(st_ref)

    for h in range(RET_HEADS):
        qh = q_ref[:, h * RET_QK_DIM:(h + 1) * RET_QK_DIM]
        kth = kt_ref[h * RET_QK_DIM:(h + 1) * RET_QK_DIM, :]
        vh = v_ref[:, h * RET_V_DIM:(h + 1) * RET_V_DIM]
        scores = _dot(qh, kth) * dmat_ref[h]
        inner = _dot(scores.astype(BF16), vh)
        s_prev = st_ref[h]
        qdec = (qh.astype(F32) * qd_ref[h]).astype(BF16)
        cross = _dot(qdec, s_prev.astype(BF16))
        kdec = (kth.astype(F32) * kdt_ref[h]).astype(BF16)
        st_ref[h] = s_prev * cd_ref[h] + _dot(kdec, vh)
        o = inner + cross
        mu = jnp.mean(o, axis=-1, keepdims=True)
        oc = o - mu
        var = jnp.mean(oc * oc, axis=-1, keepdims=True)
        sl = slice(h * RET_V_DIM, (h + 1) * RET_V_DIM)
        on = oc * lax.rsqrt(var + NORM_EPS) * gw_ref[:, sl] + gb_ref[:, sl]
        y_ref[:, sl] = (_silu(g_ref[:, sl].astype(F32)) * on).astype(y_ref.dtype)
    hn = h_ref[...] + _dot(y_ref[...], wo_ref[...])
    if final_norm:
        ms = jnp.mean(hn * hn, axis=-1, keepdims=True)
        hn = hn * lax.rsqrt(ms + NORM_EPS) * fw_ref[...]
    o_ref[...] = hn


def _retention(q, kt, v, g, h2d, dmat, qd, kdt, cd, gn_w, gn_b, w_out, final_w, batch, seq_len, final_norm):
    rt = dmat.shape[1]
    n_chunks = seq_len // rt
    n, d = h2d.shape
    row = lambda b, c: (b * n_chunks + c, 0)
    col = lambda b, c: (0, b * n_chunks + c)
    c2 = lambda b, c: (0, 0)
    c3 = lambda b, c: (0, 0, 0)
    blk = (2 * _nbytes((rt, D_MODEL), BF16) + 2 * _nbytes((rt, D_INNER), BF16) + 2 * _nbytes((rt, d), F32)
           + _nbytes(dmat.shape, F32) + 2 * _nbytes(qd.shape, F32) + _nbytes(cd.shape, F32)
           + _nbytes(w_out.shape, BF16) + 3 * _nbytes((SUBLANES, D_INNER), F32))
    scratch = _nbytes((RET_HEADS, RET_QK_DIM, RET_V_DIM), F32) + _nbytes((rt, D_INNER), BF16)
    return pl.pallas_call(
        functools.partial(_ret_kernel, final_norm=final_norm), grid=(batch, n_chunks),
        in_specs=[pl.BlockSpec((rt, D_MODEL), row), pl.BlockSpec((D_MODEL, rt), col),
                  pl.BlockSpec((rt, D_INNER), row), pl.BlockSpec((rt, D_INNER), row),
                  pl.BlockSpec((rt, d), row),
                  pl.BlockSpec(dmat.shape, c3), pl.BlockSpec(qd.shape, c3), pl.BlockSpec(kdt.shape, c3),
                  pl.BlockSpec(cd.shape, c3), pl.BlockSpec((1, D_INNER), c2), pl.BlockSpec((1, D_INNER), c2),
                  pl.BlockSpec(w_out.shape, c2), pl.BlockSpec((1, d), c2)],
        out_specs=pl.BlockSpec((rt, d), row),
        out_shape=jax.ShapeDtypeStruct((n, d), F32),
        scratch_shapes=[pltpu.VMEM((RET_HEADS, RET_QK_DIM, RET_V_DIM), F32),
                        pltpu.VMEM((rt, D_INNER), BF16)],
        compiler_params=_params(("arbitrary", "arbitrary"), blk, scratch),
        name="retention")(q, kt, v, g, h2d, dmat, qd, kdt, cd, gn_w, gn_b, w_out, final_w)


def _s5_ssd_layer(h2d, norm_w, batch, seq_len, w_in, lam_re, lam_im, log_dt, b_re, b_im, c_re, c_im,
                  s5_d, glu_w, glu_b, conv_w, conv_b, dt_bias, a_log, ssd_d, ssd_norm_w, w_out):
    n = batch * seq_len
    o1 = D_INNER
    o2 = o1 + S5_WIDTH
    o3 = o2 + SSD_CONV_DIM
    w_z = w_in[:, :o1].astype(BF16)
    w_u = w_in[:, o1:o2].astype(BF16)
    w_x = w_in[:, o2:o3].astype(BF16)
    w_dt = jnp.pad(w_in[:, o3:], ((0, 0), (0, LANES - SSD_HEADS))).astype(BF16)
    z, u, xbc, dt = _norm_proj(h2d, norm_w, [w_z, w_u, w_x, w_dt], [BF16, BF16, BF16, F32])

    ts = S5_CHUNK
    nc_s = seq_len // ts
    rcat, apcat, ftab, bttab, atab, cttab, sc = _s5_prepare(lam_re, lam_im, log_dt, b_re, b_im, c_re, c_im, nc_s)
    taps2 = _s5_taps(rcat, apcat).reshape(S5_GROUPS, S5_GROUP * S5_GROUP // 2, 2 * ts)
    u2 = u.reshape(batch * nc_s, ts, S5_WIDTH).transpose(0, 2, 1).reshape(batch * nc_s, S5_WIDTH * ts)
    y2 = _s5_core(u2, taps2, ftab, bttab, atab, cttab, sc, nc_s)
    ys = y2.reshape(batch * nc_s, S5_WIDTH, ts).transpose(0, 2, 1).reshape(n, S5_WIDTH)

    pad_h = LANES - SSD_HEADS
    cw = jnp.pad(conv_w.astype(F32), ((0, SUBLANES - SSD_CONV), (0, 0)))
    cb = conv_b.astype(F32).reshape(1, SSD_CONV_DIM)
    dtb = jnp.pad(dt_bias.astype(F32), (0, pad_h)).reshape(1, LANES)
    a_neg = jnp.pad(-jnp.exp(a_log.astype(F32)), (0, pad_h)).reshape(1, LANES)
    d_e = jnp.repeat(ssd_d.astype(F32), SSD_HEAD_DIM).reshape(1, SSD_WIDTH)
    nw = ssd_norm_w.astype(F32).reshape(1, SSD_WIDTH)
    yb = _ssd(xbc, z, dt, cw, cb, dtb, a_neg, d_e, nw, batch, seq_len // CHUNK)

    return _post0(ys, u, z, yb, h2d, s5_d.astype(F32).reshape(1, S5_WIDTH), glu_w.astype(BF16),
                  glu_b.astype(F32).reshape(1, S5_WIDTH), w_out[:S5_WIDTH].astype(BF16),
                  w_out[S5_WIDTH:].astype(BF16))


def _retention_layer(h2d, norm_w, batch, seq_len, w_in, gn_w, gn_b, w_out, final_w, final_norm):
    qk_w = RET_HEADS * RET_QK_DIM
    half = RET_QK_DIM // 2

    def deinterleave(w):
        return w.reshape(D_MODEL, RET_HEADS, half, 2).transpose(0, 1, 3, 2).reshape(D_MODEL, qk_w)

    w_q = deinterleave(w_in[:, :qk_w]).astype(BF16)
    w_kt = deinterleave(w_in[:, qk_w:2 * qk_w]).T.astype(BF16)
    w_v = w_in[:, 2 * qk_w:2 * qk_w + D_INNER].astype(BF16)
    w_g = w_in[:, 2 * qk_w + D_INNER:].astype(BF16)
    angle = 1.0 / (ROPE_BASE ** jnp.linspace(0.0, 1.0, half, dtype=F32))
    q, kt, v, g = _ret_proj(h2d, norm_w, angle, w_q, w_kt, w_v, w_g, seq_len)

    rt = min(RET_CHUNK, seq_len)
    log_gamma = jnp.log(1.0 - 2.0 ** (-5.0 - jnp.arange(RET_HEADS, dtype=F32)))
    p = jnp.arange(rt, dtype=F32)
    rel = p[:, None] - p[None, :]
    dmat = jnp.where(rel >= 0, jnp.exp(log_gamma[:, None, None] * jnp.maximum(rel, 0.0)), 0.0)
    qd = jnp.broadcast_to(jnp.exp(log_gamma[:, None] * (p[None, :] + 1.0))[..., None],
                          (RET_HEADS, rt, RET_QK_DIM))
    kdt = jnp.broadcast_to(jnp.exp(log_gamma[:, None] * (rt - 1.0 - p)[None, :])[:, None, :],
                           (RET_HEADS, RET_QK_DIM, rt))
    cd = jnp.broadcast_to(jnp.exp(log_gamma * rt)[:, None, None], (RET_HEADS, 1, RET_V_DIM))
    return _retention(q, kt, v, g, h2d, dmat, qd, kdt, cd, gn_w.astype(F32).reshape(1, D_INNER),
                      gn_b.astype(F32).reshape(1, D_INNER), w_out.astype(BF16),
                      final_w.astype(F32).reshape(1, D_MODEL), batch, seq_len, final_norm)


def kernel(x, layer_norm_w, ab_w_in, s5_lam_re, s5_lam_im, s5_log_dt, s5_b_re, s5_b_im, s5_c_re, s5_c_im,
           s5_d, s5_glu_w, s5_glu_b, ssd_conv_w, ssd_conv_b, ssd_dt_bias, ssd_a_log, ssd_d, ssd_norm_w,
           ab_w_out, ret_w_in, ret_gn_w, ret_gn_b, ret_w_out, final_norm_w):
    batch, seq_len, d = x.shape
    depth = layer_norm_w.shape[0]
    assert depth % 2 == 0 and seq_len % CHUNK == 0 and d == D_MODEL
    h = x.astype(F32).reshape(batch * seq_len, d)
    for i in range(depth):
        j = i // 2
        if i % 2 == 0:
            h = _s5_ssd_layer(h, layer_norm_w[i], batch, seq_len, ab_w_in[j], s5_lam_re[j], s5_lam_im[j],
                              s5_log_dt[j], s5_b_re[j], s5_b_im[j], s5_c_re[j], s5_c_im[j], s5_d[j],
                              s5_glu_w[j], s5_glu_b[j], ssd_conv_w[j], ssd_conv_b[j], ssd_dt_bias[j],
                              ssd_a_log[j], ssd_d[j], ssd_norm_w[j], ab_w_out[j])
        else:
            h = _retention_layer(h, layer_norm_w[i], batch, seq_len, ret_w_in[j], ret_gn_w[j], ret_gn_b[j],
                                 ret_w_out[j], final_norm_w, final_norm=(i == depth - 1))
    return h.reshape(batch, seq_len, d).astype(x.dtype)
```

```python
import functools
import math

import jax
import jax.numpy as jnp
from jax import lax
from jax.experimental import pallas as pl
from jax.experimental.pallas import tpu as pltpu

F32 = jnp.float32
BF16 = jnp.bfloat16

D_MODEL = 1024
D_INNER = 2048
CHUNK = 128
NORM_EPS = 1e-6
S5_WIDTH = 1024
S5_GROUP = 16
S5_GROUPS = 64
S5_STATE = 64
SSD_WIDTH = 1024
SSD_HEAD_DIM = 64
SSD_HEADS = 16
SSD_GROUPS = 2
SSD_REP = 8
SSD_STATE = 128
SSD_CONV = 4
SSD_CONV_DIM = 1536
RET_HEADS = 4
RET_QK_DIM = 256
RET_V_DIM = 512
ROPE_BASE = 10000.0

LANES = 128
SUBLANES = 8
MXU_DIM = 256
VMEM_CAP = 60 * 1024 * 1024
TOKEN_TILE = 512
S5_CHUNK = 64
RET_CHUNK = 256


def _params(semantics, block_bytes, scratch_bytes=0):
    need = 2 * block_bytes + scratch_bytes + 16 * 1024 * 1024
    return pltpu.CompilerParams(dimension_semantics=semantics,
                                vmem_limit_bytes=int(min(need, VMEM_CAP)))


def _nbytes(shape, dtype):
    return math.prod(shape) * jnp.dtype(dtype).itemsize


def _sigmoid(x):
    return 0.5 + 0.5 * jnp.tanh(0.5 * x)


def _silu(x):
    return x * _sigmoid(x)


def _split3(x):
    hi = x.astype(BF16)
    r1 = x - hi.astype(F32)
    mid = r1.astype(BF16)
    lo = (r1 - mid.astype(F32)).astype(BF16)
    return hi, mid, lo


def _dot(a, b):
    return jnp.dot(a, b, preferred_element_type=F32)


def _dot_nt(a, b):
    return lax.dot_general(a, b, (((1,), (1,)), ((), ())), preferred_element_type=F32)


def _rmsnorm_bf16(x, w):
    ms = jnp.mean(x * x, axis=-1, keepdims=True)
    return (x * lax.rsqrt(ms + NORM_EPS) * w).astype(BF16)


def _norm_proj_kernel(*refs, n_out):
    x_ref, nw_ref = refs[0], refs[1]
    w_refs = refs[2:2 + n_out]
    o_refs = refs[2 + n_out:2 + 2 * n_out]
    xn = _rmsnorm_bf16(x_ref[...], nw_ref[...])
    for w_ref, o_ref in zip(w_refs, o_refs):
        o_ref[...] = _dot(xn, w_ref[...]).astype(o_ref.dtype)


def _norm_proj(x2d, norm_w, weights, out_dtypes):
    n, d = x2d.shape
    tm = min(TOKEN_TILE, n)
    row = lambda i: (i, 0)
    const = lambda i: (0, 0)
    in_specs = [pl.BlockSpec((tm, d), row), pl.BlockSpec((1, d), const)]
    blk = _nbytes((tm, d), F32)
    out_specs, out_shape = [], []
    for w, dt in zip(weights, out_dtypes):
        in_specs.append(pl.BlockSpec(w.shape, const))
        out_specs.append(pl.BlockSpec((tm, w.shape[1]), row))
        out_shape.append(jax.ShapeDtypeStruct((n, w.shape[1]), dt))
        blk += _nbytes(w.shape, w.dtype) + _nbytes((tm, w.shape[1]), dt)
    return pl.pallas_call(
        functools.partial(_norm_proj_kernel, n_out=len(weights)), grid=(n // tm,),
        in_specs=in_specs, out_specs=out_specs, out_shape=out_shape,
        compiler_params=_params(("parallel",), blk), name="norm_proj")(
            x2d, norm_w.reshape(1, d), *weights)


def _ret_proj_kernel(x_ref, nw_ref, cr_ref, sr_ref, cb_ref, sb_ref, crt_ref, srt_ref, cbt_ref, sbt_ref,
                     wq_ref, wkt_ref, wv_ref, wg_ref, q_ref, kt_ref, v_ref, g_ref):
    xn = _rmsnorm_bf16(x_ref[...], nw_ref[...])
    half = RET_QK_DIM // 2
    accq = _dot(xn, wq_ref[...])
    cos = cb_ref[...] * cr_ref[...] - sb_ref[...] * sr_ref[...]
    sin = sb_ref[...] * cr_ref[...] + cb_ref[...] * sr_ref[...]
    for h in range(RET_HEADS):
        x1 = accq[:, h * RET_QK_DIM:h * RET_QK_DIM + half]
        x2 = accq[:, h * RET_QK_DIM + half:(h + 1) * RET_QK_DIM]
        q_ref[:, h * RET_QK_DIM:h * RET_QK_DIM + half] = (x1 * cos - x2 * sin).astype(q_ref.dtype)
        q_ref[:, h * RET_QK_DIM + half:(h + 1) * RET_QK_DIM] = (x2 * cos + x1 * sin).astype(q_ref.dtype)
    acck = _dot_nt(wkt_ref[...], xn)
    reps = crt_ref.shape[1] // LANES
    cbt = jnp.concatenate([cbt_ref[...]] * reps, axis=1)
    sbt = jnp.concatenate([sbt_ref[...]] * reps, axis=1)
    cost = cbt * crt_ref[...] - sbt * srt_ref[...]
    sint = sbt * crt_ref[...] + cbt * srt_ref[...]
    scale = RET_QK_DIM ** -0.5
    for h in range(RET_HEADS):
        x1 = acck[h * RET_QK_DIM:h * RET_QK_DIM + half, :]
        x2 = acck[h * RET_QK_DIM + half:(h + 1) * RET_QK_DIM, :]
        kt_ref[h * RET_QK_DIM:h * RET_QK_DIM + half, :] = ((x1 * cost - x2 * sint) * scale).astype(kt_ref.dtype)
        kt_ref[h * RET_QK_DIM + half:(h + 1) * RET_QK_DIM, :] = ((x2 * cost + x1 * sint) * scale).astype(kt_ref.dtype)
    v_ref[...] = _dot(xn, wv_ref[...]).astype(v_ref.dtype)
    g_ref[...] = _dot(xn, wg_ref[...]).astype(g_ref.dtype)


def _ret_proj(x2d, norm_w, angle, w_q, w_kt, w_v, w_g, seq_len):
    n, d = x2d.shape
    tm = min(TOKEN_TILE, seq_len)
    tiles_per_seq = seq_len // tm
    half = angle.shape[0]
    qk_w = w_q.shape[1]
    row = lambda i: (i, 0)
    col = lambda i: (0, i)
    const = lambda i: (0, 0)
    th_r = jnp.arange(tm, dtype=F32)[:, None] * angle[None, :]
    th_b = (jnp.arange(tiles_per_seq, dtype=F32) * tm)[:, None] * angle[None, :]
    cos_r, sin_r = jnp.cos(th_r), jnp.sin(th_r)
    cos_b, sin_b = jnp.cos(th_b)[:, None, :], jnp.sin(th_b)[:, None, :]
    cos_bt = jnp.broadcast_to(jnp.cos(th_b)[:, :, None], (tiles_per_seq, half, LANES))
    sin_bt = jnp.broadcast_to(jnp.sin(th_b)[:, :, None], (tiles_per_seq, half, LANES))
    base = lambda i: (i % tiles_per_seq, 0, 0)
    blk = (_nbytes((tm, d), F32) + 4 * _nbytes((tm, half), F32) + 2 * _nbytes((half, LANES), F32)
           + 2 * _nbytes(w_q.shape, BF16)
           + 2 * _nbytes(w_v.shape, BF16) + 2 * _nbytes((tm, qk_w), BF16) + 2 * _nbytes((tm, D_INNER), BF16))
    return pl.pallas_call(
        _ret_proj_kernel, grid=(n // tm,),
        in_specs=[pl.BlockSpec((tm, d), row), pl.BlockSpec((1, d), const),
                  pl.BlockSpec((tm, half), const), pl.BlockSpec((tm, half), const),
                  pl.BlockSpec((None, 1, half), base), pl.BlockSpec((None, 1, half), base),
                  pl.BlockSpec((half, tm), const), pl.BlockSpec((half, tm), const),
                  pl.BlockSpec((None, half, LANES), base), pl.BlockSpec((None, half, LANES), base),
                  pl.BlockSpec(w_q.shape, const), pl.BlockSpec(w_kt.shape, const),
                  pl.BlockSpec(w_v.shape, const), pl.BlockSpec(w_g.shape, const)],
        out_specs=[pl.BlockSpec((tm, qk_w), row), pl.BlockSpec((qk_w, tm), col),
                   pl.BlockSpec((tm, D_INNER), row), pl.BlockSpec((tm, D_INNER), row)],
        out_shape=[jax.ShapeDtypeStruct((n, qk_w), BF16), jax.ShapeDtypeStruct((qk_w, n), BF16),
                   jax.ShapeDtypeStruct((n, D_INNER), BF16), jax.ShapeDtypeStruct((n, D_INNER), BF16)],
        compiler_params=_params(("parallel",), blk), name="ret_proj")(
            x2d, norm_w.reshape(1, d), cos_r, sin_r, cos_b, sin_b, cos_r.T, sin_r.T, cos_bt, sin_bt,
            w_q, w_kt, w_v, w_g)


def _s5_taps_kernel(r_ref, a_ref, k_ref):
    k_ref[...] = jnp.dot(r_ref[...], a_ref[...], preferred_element_type=F32,
                         precision=lax.Precision.HIGHEST)


def _s5_taps(rcat, apcat):
    g, m, kk = rcat.shape
    t = apcat.shape[2]
    return pl.pallas_call(
        _s5_taps_kernel, grid=(g,),
        in_specs=[pl.BlockSpec((None, m, kk), lambda i: (i, 0, 0)),
                  pl.BlockSpec((None, kk, t), lambda i: (i, 0, 0))],
        out_specs=pl.BlockSpec((None, m, t), lambda i: (i, 0, 0)),
        out_shape=jax.ShapeDtypeStruct((g, m, t), F32),
        compiler_params=_params(("parallel",), _nbytes((m, kk + t), F32) + _nbytes((kk, t), F32)),
        name="s5_taps")(rcat, apcat)


def _s5_core_kernel(u_ref, k_ref, f_ref, bt_ref, a_ref, ct_ref, sc_ref, y_ref, w_ref, *, n_chunks):
    ts = S5_CHUNK
    h = S5_GROUP
    half = S5_STATE
    rows = u_ref.shape[0]
    u = u_ref[...]

    f1 = f_ref[0]
    f2 = f_ref[1]
    pmat = jnp.concatenate(
        [(f1 * bt_ref[0, hi:hi + 1, :] + f2 * bt_ref[1, hi:hi + 1, :]).astype(BF16) for hi in range(h)], axis=0)
    e = _dot(u, pmat)

    row_idx = lax.broadcasted_iota(jnp.int32, (n_chunks, 2 * half), 0)
    hprev = []
    for b in range(rows // n_chunks):
        acc = e[b * n_chunks:(b + 1) * n_chunks, :]
        acc_sw = pltpu.roll(acc, half, 1)
        d = 1
        lvl = 0
        while d < n_chunks:
            c1 = sc_ref[2 * lvl:2 * lvl + 1, :]
            c2 = sc_ref[2 * lvl + 1:2 * lvl + 2, :]
            sh = jnp.where(row_idx >= d, pltpu.roll(acc, d, 0), 0.0)
            sh_sw = jnp.where(row_idx >= d, pltpu.roll(acc_sw, d, 0), 0.0)
            acc = acc + c1 * sh + c2 * sh_sw
            acc_sw = acc_sw + c1 * sh_sw - c2 * sh
            d *= 2
            lvl += 1
        hprev.append(jnp.where(row_idx >= 1, pltpu.roll(acc, 1, 0), 0.0))
    hp = jnp.concatenate(hprev, axis=0).astype(BF16)

    s_idx = lax.broadcasted_iota(jnp.int32, (ts, LANES), 0)
    l_idx = lax.broadcasted_iota(jnp.int32, (ts, LANES), 1)
    keep = (l_idx & (ts - 1)) >= s_idx
    first = lax.broadcasted_iota(jnp.int32, (2 * half, LANES), 1) < ts
    slabs_per_tile = MXU_DIM // LANES
    kdim = h * ts
    for n in range(h * ts // MXU_DIM):
        for j in range(slabs_per_tile):
            m = n * slabs_per_tile + j
            cols = slice(j * LANES, (j + 1) * LANES)
            for hi in range(h):
                kb = jnp.broadcast_to(k_ref[hi * (h // 2) + m:hi * (h // 2) + m + 1, :], (ts, LANES))
                tz = pltpu.roll(kb, 0, 1, stride=1, stride_axis=0)
                w_ref[n, hi * ts:(hi + 1) * ts, cols] = jnp.where(keep, tz, 0.0).astype(BF16)
            k1 = jnp.where(first, jnp.broadcast_to(ct_ref[0, :, 2 * m:2 * m + 1], (2 * half, LANES)),
                           jnp.broadcast_to(ct_ref[0, :, 2 * m + 1:2 * m + 2], (2 * half, LANES)))
            k2 = jnp.where(first, jnp.broadcast_to(ct_ref[1, :, 2 * m:2 * m + 1], (2 * half, LANES)),
                           jnp.broadcast_to(ct_ref[1, :, 2 * m + 1:2 * m + 2], (2 * half, LANES)))
            w_ref[n, kdim:kdim + 2 * half, cols] = (k1 * a_ref[0] + k2 * a_ref[1]).astype(BF16)
        y = _dot(u, w_ref[n, 0:kdim, :]) + _dot(hp, w_ref[n, kdim:kdim + 2 * half, :])
        y_ref[:, n * MXU_DIM:(n + 1) * MXU_DIM] = y.astype(y_ref.dtype)


def _s5_core(u2, taps2, ftab, bttab, atab, cttab, sc, n_chunks):
    rows = u2.shape[0]
    g = taps2.shape[0]
    w = S5_GROUP * S5_CHUNK
    tabs = [taps2, ftab, bttab, atab, cttab, sc]
    blk = 2 * _nbytes((rows, w), BF16) + sum(_nbytes(t.shape[1:], F32) for t in tabs)
    wshape = (w // MXU_DIM, w + 2 * S5_STATE, MXU_DIM)

    def tab_spec(t):
        nd = t.ndim - 1
        return pl.BlockSpec((None,) + t.shape[1:], lambda i: (i,) + (0,) * nd)

    return pl.pallas_call(
        functools.partial(_s5_core_kernel, n_chunks=n_chunks), grid=(g,),
        in_specs=[pl.BlockSpec((rows, w), lambda i: (0, i))] + [tab_spec(t) for t in tabs],
        out_specs=pl.BlockSpec((rows, w), lambda i: (0, i)),
        out_shape=jax.ShapeDtypeStruct((rows, g * w), BF16),
        scratch_shapes=[pltpu.VMEM(wshape, BF16)],
        compiler_params=_params(("parallel",), blk, _nbytes(wshape, BF16)),
        name="s5_core")(u2, *tabs)


def _s5_prepare(lam_re, lam_im, log_dt, b_re, b_im, c_re, c_im, n_chunks):
    ts = S5_CHUNK
    assert 2 * ts == LANES and 2 * S5_STATE == LANES
    dt = jnp.exp(log_dt.astype(F32))[:, None]
    lr = jnp.minimum(lam_re.astype(F32), -1e-4)
    li = lam_im.astype(F32)
    mag = jnp.exp(lr * dt)
    ab_re = mag * jnp.cos(li * dt)
    ab_im = mag * jnp.sin(li * dt)
    den = lr * lr + li * li
    nr = ab_re - 1.0
    coef_re = (nr * lr + ab_im * li) / den
    coef_im = (ab_im * lr - nr * li) / den
    br = b_re.astype(F32)
    bi = b_im.astype(F32)
    bb_re = coef_re[..., None] * br - coef_im[..., None] * bi
    bb_im = coef_re[..., None] * bi + coef_im[..., None] * br
    cr = c_re.astype(F32)
    ci = c_im.astype(F32)
    g = lr.shape[0]
    j = jnp.arange(ts + 1, dtype=F32)
    pmag = jnp.exp((lr * dt)[..., None] * j)
    ang = (li * dt)[..., None] * j
    pw_re = pmag * jnp.cos(ang)
    pw_im = pmag * jnp.sin(ang)
    bbt_re = bb_re.transpose(0, 2, 1)
    bbt_im = bb_im.transpose(0, 2, 1)
    r_re = cr[:, None] * bbt_re[:, :, None] - ci[:, None] * bbt_im[:, :, None]
    r_im = cr[:, None] * bbt_im[:, :, None] + ci[:, None] * bbt_re[:, :, None]
    rcat = jnp.concatenate([r_re, -r_im], axis=-1).reshape(g, S5_GROUP * S5_GROUP, 2 * S5_STATE)
    apcat = jnp.concatenate([pw_re[..., :ts], pw_im[..., :ts]], axis=1)
    fr = pw_re[..., ts - 1::-1].transpose(0, 2, 1)
    fi = pw_im[..., ts - 1::-1].transpose(0, 2, 1)
    ftab = jnp.stack([jnp.concatenate([fr, fr], -1), jnp.concatenate([-fi, fi], -1)], axis=1)
    bttab = jnp.stack([jnp.concatenate([bbt_re, bbt_im], -1), jnp.concatenate([bbt_im, bbt_re], -1)], axis=1)
    wr = pw_re[..., 1:]
    wi = pw_im[..., 1:]
    a1 = jnp.concatenate([wr, -wi], axis=1)
    a2 = jnp.concatenate([-wi, -wr], axis=1)
    atab = jnp.stack([jnp.concatenate([a1, a1], -1), jnp.concatenate([a2, a2], -1)], axis=1)
    crt = cr.transpose(0, 2, 1)
    cit = ci.transpose(0, 2, 1)
    cttab = jnp.stack([jnp.concatenate([crt, crt], 1), jnp.concatenate([cit, cit], 1)], axis=1)
    rows = []
    d = 1
    while d < n_chunks:
        ex = float(ts * d)
        m = jnp.exp(lr * dt * ex)
        a_r = m * jnp.cos(li * dt * ex)
        a_i = m * jnp.sin(li * dt * ex)
        rows.append(jnp.concatenate([a_r, a_r], axis=-1))
        rows.append(jnp.concatenate([-a_i, a_i], axis=-1))
        d *= 2
    while len(rows) % SUBLANES or not rows:
        rows.append(jnp.zeros((g, 2 * S5_STATE), F32))
    sc = jnp.stack(rows, axis=1)
    return rcat, apcat, ftab, bttab, atab, cttab, sc


def _ssd_kernel(xbc_ref, z_ref, dt_ref, shift_ref, tri_ref, exp_ref, cw_ref, cb_ref, dtb_ref, a_ref,
                dsk_ref, nw_ref, y_ref, xcat_ref, st_ref, yg_ref):
    t = CHUNK
    c = pl.program_id(0)

    @pl.when(c == 0)
    def _():
        xcat_ref[:, 0:t, :] = jnp.zeros((xcat_ref.shape[0], t, SSD_CONV_DIM), BF16)
        st_ref[...] = jnp.zeros_like(st_ref)

    @pl.when(c != 0)
    def _():
        xcat_ref[:, 0:t, :] = xcat_ref[:, t:2 * t, :]

    for b in range(xbc_ref.shape[0]):
        _ssd_chunk(xbc_ref.at[b], z_ref.at[b], dt_ref.at[b], shift_ref, tri_ref, exp_ref, cw_ref, cb_ref,
                   dtb_ref, a_ref, dsk_ref, nw_ref, y_ref.at[b], xcat_ref.at[b],
                   st_ref.at[pl.ds(b * SSD_GROUPS, SSD_GROUPS)], yg_ref.at[b])


def _ssd_chunk(xbc_ref, z_ref, dt_ref, shift_ref, tri_ref, exp_ref, cw_ref, cb_ref, dtb_ref, a_ref,
               dsk_ref, nw_ref, y_ref, xcat_ref, st_ref, yg_ref):
    t = CHUNK
    xcat_ref[t:2 * t, :] = xbc_ref[...]
    xsh = _dot(shift_ref[...], xcat_ref[...])
    acc = cb_ref[...] + cw_ref[SSD_CONV - 1:SSD_CONV, :] * xbc_ref[...].astype(F32)
    for k in range(SSD_CONV - 1):
        acc = acc + cw_ref[k:k + 1, :] * xsh[k * t:(k + 1) * t, :]
    xc = _silu(acc)
    gn = SSD_GROUPS * SSD_STATE
    xs = xc[:, :SSD_WIDTH]
    bm = xc[:, SSD_WIDTH:SSD_WIDTH + gn]
    cm = xc[:, SSD_WIDTH + gn:]

    xr = dt_ref[...] + dtb_ref[...]
    dt = jnp.maximum(xr, 0.0) + jnp.log(1.0 + jnp.exp(-jnp.abs(xr)))
    da = dt * a_ref[...]
    cs3 = _dot(tri_ref[...], jnp.concatenate(_split3(da), axis=1))
    a_cs = cs3[:, :LANES] + cs3[:, LANES:2 * LANES] + cs3[:, 2 * LANES:]
    expand = exp_ref[...]
    dt_e = _dot(jnp.concatenate(_split3(dt)[:2], axis=1), expand)
    acs_e = _dot(jnp.concatenate(_split3(a_cs)[:2], axis=1), expand)
    last_e = acs_e[t - 1:t, :]
    xdt = xs * dt_e
    xdt_b = xdt.astype(BF16)
    xd_b = (xdt * jnp.exp(last_e - acs_e)).astype(BF16)
    exp_acs = jnp.exp(acs_e)
    chunk_decay = jnp.exp(last_e)
    a_cs_t = a_cs.T
    r_i = lax.broadcasted_iota(jnp.int32, (t, t), 0)
    c_i = lax.broadcasted_iota(jnp.int32, (t, t), 1)
    causal = r_i >= c_i
    lane = lax.broadcasted_iota(jnp.int32, (t, LANES), 1)
    gw = SSD_REP * SSD_HEAD_DIM
    for g in range(SSD_GROUPS):
        bg = bm[:, g * SSD_STATE:(g + 1) * SSD_STATE]
        cg = cm[:, g * SSD_STATE:(g + 1) * SSD_STATE].astype(BF16)
        gmat = _dot_nt(cg, bg.astype(BF16))
        s_prev = st_ref[g]
        y_off = _dot(cg, s_prev.astype(BF16)) * exp_acs[:, g * gw:(g + 1) * gw]
        st_ref[g] = (s_prev * chunk_decay[:, g * gw:(g + 1) * gw]
                     + _dot(bg.T.astype(BF16), xd_b[:, g * gw:(g + 1) * gw]))
        for pr in range(SSD_REP // 2):
            ms = []
            for hh in range(2):
                hd = g * SSD_REP + 2 * pr + hh
                seg = a_cs[:, hd:hd + 1] - a_cs_t[hd:hd + 1, :]
                ms.append((gmat * jnp.exp(jnp.where(causal, seg, -jnp.inf))).astype(BF16))
            lo = g * gw + pr * LANES
            x2 = xdt_b[:, lo:lo + LANES]
            rhs = jnp.concatenate([jnp.where(lane < SSD_HEAD_DIM, x2, jnp.zeros_like(x2)),
                                   jnp.where(lane >= SSD_HEAD_DIM, x2, jnp.zeros_like(x2))], axis=0)
            y_d = _dot(jnp.concatenate(ms, axis=1), rhs)
            y = y_d + y_off[:, pr * LANES:(pr + 1) * LANES] + xs[:, lo:lo + LANES] * dsk_ref[:, lo:lo + LANES]
            yg_ref[:, lo:lo + LANES] = y * _silu(z_ref[:, lo:lo + LANES].astype(F32))
    for g in range(SSD_GROUPS):
        yg = yg_ref[:, g * gw:(g + 1) * gw]
        ms_ = jnp.mean(yg * yg, axis=-1, keepdims=True)
        y_ref[:, g * gw:(g + 1) * gw] = (yg * lax.rsqrt(ms_ + NORM_EPS)
                                         * nw_ref[:, g * gw:(g + 1) * gw]).astype(y_ref.dtype)


def _ssd(xbc, z, dt, conv_w, conv_b, dt_bias, a_neg, d_e, norm_w, batch, n_chunks):
    t = CHUNK
    n = xbc.shape[0]
    seq_len = n // batch
    row = lambda c: (0, c, 0)
    const = lambda c: (0, 0)
    xbc = xbc.reshape(batch, seq_len, SSD_CONV_DIM)
    z = z.reshape(batch, seq_len, z.shape[1])
    dt = dt.reshape(batch, seq_len, LANES)
    rr = jnp.arange((SSD_CONV - 1) * t)
    src = t + rr % t - (SSD_CONV - 1) + rr // t
    shift = (jnp.arange(2 * t)[None, :] == src[:, None]).astype(BF16)
    tri = (jnp.arange(t)[:, None] >= jnp.arange(t)[None, :]).astype(BF16)
    expand = (jnp.arange(SSD_WIDTH)[None, :] // SSD_HEAD_DIM
              == (jnp.arange(2 * LANES) % LANES)[:, None]).astype(BF16)
    blk = (batch * (_nbytes((t, SSD_CONV_DIM), BF16) + 2 * _nbytes((t, SSD_WIDTH), BF16)
                    + _nbytes((t, LANES), F32))
           + _nbytes(shift.shape, BF16) + _nbytes(tri.shape, BF16) + _nbytes(expand.shape, BF16)
           + 6 * _nbytes((SUBLANES, SSD_CONV_DIM), F32))
    st_shape = (batch * SSD_GROUPS, SSD_STATE, SSD_REP * SSD_HEAD_DIM)
    scratch = (_nbytes((batch, 2 * t, SSD_CONV_DIM), BF16) + _nbytes(st_shape, F32)
               + _nbytes((batch, t, SSD_WIDTH), F32))
    out = pl.pallas_call(
        _ssd_kernel, grid=(n_chunks,),
        in_specs=[pl.BlockSpec((batch, t, SSD_CONV_DIM), row),
                  pl.BlockSpec((batch, t, SSD_WIDTH), lambda c: (0, c, 1)),
                  pl.BlockSpec((batch, t, LANES), row),
                  pl.BlockSpec(shift.shape, const), pl.BlockSpec(tri.shape, const),
                  pl.BlockSpec(expand.shape, const),
                  pl.BlockSpec(conv_w.shape, const), pl.BlockSpec(conv_b.shape, const),
                  pl.BlockSpec(dt_bias.shape, const), pl.BlockSpec(a_neg.shape, const),
                  pl.BlockSpec(d_e.shape, const), pl.BlockSpec(norm_w.shape, const)],
        out_specs=pl.BlockSpec((batch, t, SSD_WIDTH), row),
        out_shape=jax.ShapeDtypeStruct((batch, seq_len, SSD_WIDTH), BF16),
        scratch_shapes=[pltpu.VMEM((batch, 2 * t, SSD_CONV_DIM), BF16),
                        pltpu.VMEM(st_shape, F32),
                        pltpu.VMEM((batch, t, SSD_WIDTH), F32)],
        compiler_params=_params(("arbitrary",), blk, scratch),
        name="ssd")(xbc, z, dt, shift, tri, expand, conv_w, conv_b, dt_bias, a_neg, d_e, norm_w)
    return out.reshape(n, SSD_WIDTH)


def _post0_kernel(ys_ref, u_ref, z_ref, yb_ref, x_ref, d_ref, gw_ref, gb_ref, wa_ref, wb_ref, o_ref):
    u = u_ref[...].astype(F32)
    v = ys_ref[...].astype(F32) + d_ref[...] * u
    y = 0.5 * v * (1.0 + jnp.tanh(math.sqrt(2.0 / math.pi) * (v + 0.044715 * (v * v * v))))
    gl = _dot(y.astype(BF16), gw_ref[...]) + gb_ref[...]
    ya = y * _sigmoid(gl) * _silu(z_ref[...].astype(F32))
    out = _dot(ya.astype(BF16), wa_ref[...]) + _dot(yb_ref[...].astype(BF16), wb_ref[...])
    o_ref[...] = x_ref[...] + out


def _post0(ys, u, z, yb, x2d, s5_d, glu_w, glu_b, w_a, w_b):
    n, d = x2d.shape
    tm = min(TOKEN_TILE, n)
    row = lambda i: (i, 0)
    const = lambda i: (0, 0)
    blk = (3 * _nbytes((tm, d), BF16) + 3 * _nbytes((tm, d), F32) + 3 * _nbytes((d, d), BF16)
           + 2 * _nbytes((SUBLANES, d), F32))
    return pl.pallas_call(
        _post0_kernel, grid=(n // tm,),
        in_specs=[pl.BlockSpec((tm, d), row), pl.BlockSpec((tm, d), row), pl.BlockSpec((tm, d), row),
                  pl.BlockSpec((tm, d), row), pl.BlockSpec((tm, d), row),
                  pl.BlockSpec((1, d), const), pl.BlockSpec(glu_w.shape, const), pl.BlockSpec((1, d), const),
                  pl.BlockSpec(w_a.shape, const), pl.BlockSpec(w_b.shape, const)],
        out_specs=pl.BlockSpec((tm, d), row),
        out_shape=jax.ShapeDtypeStruct((n, d), F32),
        compiler_params=_params(("parallel",), blk), name="post0")(
            ys, u, z, yb, x2d, s5_d, glu_w, glu_b, w_a, w_b)


def _ret_kernel(*refs, n_seq, final_norm):
    q_ref = refs[0]
    kt_refs = refs[1:1 + n_seq]
    (v_ref, g_ref, h_ref, dmat_ref, qd_ref, kdt_ref, cd_ref, gw_ref, gb_ref, wo_ref, fw_ref,
     o_ref, st_ref, y_ref) = refs[1 + n_seq:]

    @pl.when(pl.program_id(0) == 0)
    def _():
        st_ref[...] = jnp.zeros_like(st_ref)

    for b in range(n_seq):
        _ret_chunk(q_ref.at[b], kt_refs[b], v_ref.at[b], g_ref.at[b], h_ref.at[b], dmat_ref, qd_ref, kdt_ref,
                   cd_ref, gw_ref, gb_ref, wo_ref, fw_ref, o_ref.at[b],
                   st_ref.at[pl.ds(b * RET_HEADS, RET_HEADS)], y_ref.at[b], final_norm=final_norm)


def _ret_chunk(q_ref, kt_ref, v_ref, g_ref, h_ref, dmat_ref, qd_ref, kdt_ref, cd_ref, gw_ref, gb_ref,
               wo_ref, fw_ref, o_ref, st_ref, y_ref, *, final_norm):
    for h in range(RET_HEADS):
        qh = q_ref[:, h * RET_QK_DIM:(h + 1) * RET_QK_DIM]
        kth = kt_ref[h * RET_QK_DIM:(h + 1) * RET_QK_DIM, :]
        vh = v_ref[:, h * RET_V_DIM:(h + 1) * RET_V_DIM]
        scores = _dot(qh, kth) * dmat_ref[h]
        inner = _dot(scores.astype(BF16), vh)
        s_prev = st_ref[h]
        qdec = (qh.astype(F32) * qd_ref[h]).astype(BF16)
        cross = _dot(qdec, s_prev.astype(BF16))
        kdec = (kth.astype(F32) * kdt_ref[h]).astype(BF16)
        st_ref[h] = s_prev * cd_ref[h] + _dot(kdec, vh)
        o = inner + cross
        mu = jnp.mean(o, axis=-1, keepdims=True)
        oc = o - mu
        var = jnp.mean(oc * oc, axis=-1, keepdims=True)
        sl = slice(h * RET_V_DIM, (h + 1) * RET_V_DIM)
        on = oc * lax.rsqrt(var + NORM_EPS) * gw_ref[:, sl] + gb_ref[:, sl]
        y_ref[:, sl] = (_silu(g_ref[:, sl].astype(F32)) * on).astype(y_ref.dtype)
    hn = h_ref[...] + _dot(y_ref[...], wo_ref[...])
    if final_norm:
        ms = jnp.mean(hn * hn, axis=-1, keepdims=True)
        hn = hn * lax.rsqrt(ms + NORM_EPS) * fw_ref[...]
    o_ref[...] = hn


def _retention(q, kt, v, g, h2d, dmat, qd, kdt, cd, gn_w, gn_b, w_out, final_w, batch, seq_len, final_norm):
    rt = dmat.shape[1]
    n_chunks = seq_len // rt
    n, d = h2d.shape
    row = lambda c: (0, c, 0)
    c2 = lambda c: (0, 0)
    c3 = lambda c: (0, 0, 0)
    seq3 = lambda a: a.reshape(batch, seq_len, a.shape[1])
    blk = (batch * (2 * _nbytes((rt, D_MODEL), BF16) + 2 * _nbytes((rt, D_INNER), BF16)
                    + 2 * _nbytes((rt, d), F32))
           + _nbytes(dmat.shape, F32) + 2 * _nbytes(qd.shape, F32) + _nbytes(cd.shape, F32)
           + _nbytes(w_out.shape, BF16) + 3 * _nbytes((SUBLANES, D_INNER), F32))
    st_shape = (batch * RET_HEADS, RET_QK_DIM, RET_V_DIM)
    scratch = _nbytes(st_shape, F32) + _nbytes((batch, rt, D_INNER), BF16)
    kt_specs = [pl.BlockSpec((D_MODEL, rt), functools.partial(lambda c, b: (0, b * n_chunks + c), b=b))
                for b in range(batch)]
    out = pl.pallas_call(
        functools.partial(_ret_kernel, n_seq=batch, final_norm=final_norm), grid=(n_chunks,),
        in_specs=[pl.BlockSpec((batch, rt, D_MODEL), row)] + kt_specs
                 + [pl.BlockSpec((batch, rt, D_INNER), row), pl.BlockSpec((batch, rt, D_INNER), row),
                    pl.BlockSpec((batch, rt, d), row),
                    pl.BlockSpec(dmat.shape, c3), pl.BlockSpec(qd.shape, c3), pl.BlockSpec(kdt.shape, c3),
                    pl.BlockSpec(cd.shape, c3), pl.BlockSpec((1, D_INNER), c2), pl.BlockSpec((1, D_INNER), c2),
                    pl.BlockSpec(w_out.shape, c2), pl.BlockSpec((1, d), c2)],
        out_specs=pl.BlockSpec((batch, rt, d), row),
        out_shape=jax.ShapeDtypeStruct((batch, seq_len, d), F32),
        scratch_shapes=[pltpu.VMEM(st_shape, F32), pltpu.VMEM((batch, rt, D_INNER), BF16)],
        compiler_params=_params(("arbitrary",), blk, scratch),
        name="retention")(seq3(q), *([kt] * batch), seq3(v), seq3(g), seq3(h2d), dmat, qd, kdt, cd,
                          gn_w, gn_b, w_out, final_w)
    return out.reshape(n, d)


def _s5_ssd_layer(h2d, norm_w, batch, seq_len, w_in, lam_re, lam_im, log_dt, b_re, b_im, c_re, c_im,
                  s5_d, glu_w, glu_b, conv_w, conv_b, dt_bias, a_log, ssd_d, ssd_norm_w, w_out):
    n = batch * seq_len
    o1 = D_INNER
    o2 = o1 + S5_WIDTH
    o3 = o2 + SSD_CONV_DIM
    w_z = w_in[:, :o1].astype(BF16)
    w_u = w_in[:, o1:o2].astype(BF16)
    w_x = w_in[:, o2:o3].astype(BF16)
    w_dt = jnp.pad(w_in[:, o3:], ((0, 0), (0, LANES - SSD_HEADS))).astype(BF16)
    z, u, xbc, dt = _norm_proj(h2d, norm_w, [w_z, w_u, w_x, w_dt], [BF16, BF16, BF16, F32])

    ts = S5_CHUNK
    nc_s = seq_len // ts
    rcat, apcat, ftab, bttab, atab, cttab, sc = _s5_prepare(lam_re, lam_im, log_dt, b_re, b_im, c_re, c_im, nc_s)
    taps2 = _s5_taps(rcat, apcat).reshape(S5_GROUPS, S5_GROUP * S5_GROUP // 2, 2 * ts)
    u2 = u.reshape(batch * nc_s, ts, S5_WIDTH).transpose(0, 2, 1).reshape(batch * nc_s, S5_WIDTH * ts)
    y2 = _s5_core(u2, taps2, ftab, bttab, atab, cttab, sc, nc_s)
    ys = y2.reshape(batch * nc_s, S5_WIDTH, ts).transpose(0, 2, 1).reshape(n, S5_WIDTH)

    pad_h = LANES - SSD_HEADS
    cw = jnp.pad(conv_w.astype(F32), ((0, SUBLANES - SSD_CONV), (0, 0)))
    cb = conv_b.astype(F32).reshape(1, SSD_CONV_DIM)
    dtb = jnp.pad(dt_bias.astype(F32), (0, pad_h)).reshape(1, LANES)
    a_neg = jnp.pad(-jnp.exp(a_log.astype(F32)), (0, pad_h)).reshape(1, LANES)
    d_e = jnp.repeat(ssd_d.astype(F32), SSD_HEAD_DIM).reshape(1, SSD_WIDTH)
    nw = ssd_norm_w.astype(F32).reshape(1, SSD_WIDTH)
    yb = _ssd(xbc, z, dt, cw, cb, dtb, a_neg, d_e, nw, batch, seq_len // CHUNK)

    return _post0(ys, u, z, yb, h2d, s5_d.astype(F32).reshape(1, S5_WIDTH), glu_w.astype(BF16),
                  glu_b.astype(F32).reshape(1, S5_WIDTH), w_out[:S5_WIDTH].astype(BF16),
                  w_out[S5_WIDTH:].astype(BF16))


def _retention_layer(h2d, norm_w, batch, seq_len, w_in, gn_w, gn_b, w_out, final_w, final_norm):
    qk_w = RET_HEADS * RET_QK_DIM
    half = RET_QK_DIM // 2

    def deinterleave(w):
        return w.reshape(D_MODEL, RET_HEADS, half, 2).transpose(0, 1, 3, 2).reshape(D_MODEL, qk_w)

    w_q = deinterleave(w_in[:, :qk_w]).astype(BF16)
    w_kt = deinterleave(w_in[:, qk_w:2 * qk_w]).T.astype(BF16)
    w_v = w_in[:, 2 * qk_w:2 * qk_w + D_INNER].astype(BF16)
    w_g = w_in[:, 2 * qk_w + D_INNER:].astype(BF16)
    angle = 1.0 / (ROPE_BASE ** jnp.linspace(0.0, 1.0, half, dtype=F32))
    q, kt, v, g = _ret_proj(h2d, norm_w, angle, w_q, w_kt, w_v, w_g, seq_len)

    rt = min(RET_CHUNK, seq_len)
    log_gamma = jnp.log(1.0 - 2.0 ** (-5.0 - jnp.arange(RET_HEADS, dtype=F32)))
    p = jnp.arange(rt, dtype=F32)
    rel = p[:, None] - p[None, :]
    dmat = jnp.where(rel >= 0, jnp.exp(log_gamma[:, None, None] * jnp.maximum(rel, 0.0)), 0.0)
    qd = jnp.broadcast_to(jnp.exp(log_gamma[:, None] * (p[None, :] + 1.0))[..., None],
                          (RET_HEADS, rt, RET_QK_DIM))
    kdt = jnp.broadcast_to(jnp.exp(log_gamma[:, None] * (rt - 1.0 - p)[None, :])[:, None, :],
                           (RET_HEADS, RET_QK_DIM, rt))
    cd = jnp.broadcast_to(jnp.exp(log_gamma * rt)[:, None, None], (RET_HEADS, 1, RET_V_DIM))
    return _retention(q, kt, v, g, h2d, dmat, qd, kdt, cd, gn_w.astype(F32).reshape(1, D_INNER),
                      gn_b.astype(F32).reshape(1, D_INNER), w_out.astype(BF16),
                      final_w.astype(F32).reshape(1, D_MODEL), batch, seq_len, final_norm)


def kernel(x, layer_norm_w, ab_w_in, s5_lam_re, s5_lam_im, s5_log_dt, s5_b_re, s5_b_im, s5_c_re, s5_c_im,
           s5_d, s5_glu_w, s5_glu_b, ssd_conv_w, ssd_conv_b, ssd_dt_bias, ssd_a_log, ssd_d, ssd_norm_w,
           ab_w_out, ret_w_in, ret_gn_w, ret_gn_b, ret_w_out, final_norm_w):
    batch, seq_len, d = x.shape
    depth = layer_norm_w.shape[0]
    assert depth % 2 == 0 and seq_len % CHUNK == 0 and d == D_MODEL
    h = x.astype(F32).reshape(batch * seq_len, d)
    for i in range(depth):
        j = i // 2
        if i % 2 == 0:
            h = _s5_ssd_layer(h, layer_norm_w[i], batch, seq_len, ab_w_in[j], s5_lam_re[j], s5_lam_im[j],
                              s5_log_dt[j], s5_b_re[j], s5_b_im[j], s5_c_re[j], s5_c_im[j], s5_d[j],
                              s5_glu_w[j], s5_glu_b[j], ssd_conv_w[j], ssd_conv_b[j], ssd_dt_bias[j],
                              ssd_a_log[j], ssd_d[j], ssd_norm_w[j], ab_w_out[j])
        else:
            h = _retention_layer(h, layer_norm_w[i], batch, seq_len, ret_w_in[j], ret_gn_w[j], ret_gn_b[j],
                                 ret_w_out[j], final_norm_w, final_norm=(i == depth - 1))
    return h.reshape(batch, seq_len, d).astype(x.dtype)
```

```python
import functools
import math

import jax
import jax.numpy as jnp
from jax import lax
from jax.experimental import pallas as pl
from jax.experimental.pallas import tpu as pltpu

F32 = jnp.float32
BF16 = jnp.bfloat16

D_MODEL = 1024
D_INNER = 2048
CHUNK = 128
NORM_EPS = 1e-6
S5_WIDTH = 1024
S5_GROUP = 16
S5_GROUPS = 64
S5_STATE = 64
SSD_WIDTH = 1024
SSD_HEAD_DIM = 64
SSD_HEADS = 16
SSD_GROUPS = 2
SSD_REP = 8
SSD_STATE = 128
SSD_CONV = 4
SSD_CONV_DIM = 1536
RET_HEADS = 4
RET_QK_DIM = 256
RET_V_DIM = 512
ROPE_BASE = 10000.0

LANES = 128
SUBLANES = 8
MXU_DIM = 256
VMEM_CAP = 60 * 1024 * 1024
TOKEN_TILE = 512
WIDE_TOKEN_TILE = 1024
S5_CHUNK = 64
RET_CHUNK = 256


def _params(semantics, block_bytes, scratch_bytes=0):
    need = 2 * block_bytes + scratch_bytes + 16 * 1024 * 1024
    return pltpu.CompilerParams(dimension_semantics=semantics,
                                vmem_limit_bytes=int(min(need, VMEM_CAP)))


def _nbytes(shape, dtype):
    return math.prod(shape) * jnp.dtype(dtype).itemsize


def _sigmoid(x):
    return 0.5 + 0.5 * jnp.tanh(0.5 * x)


def _silu(x):
    return x * _sigmoid(x)


def _split3(x):
    hi = x.astype(BF16)
    r1 = x - hi.astype(F32)
    mid = r1.astype(BF16)
    lo = (r1 - mid.astype(F32)).astype(BF16)
    return hi, mid, lo


def _dot(a, b):
    return jnp.dot(a, b, preferred_element_type=F32)


def _dot_nt(a, b):
    return lax.dot_general(a, b, (((1,), (1,)), ((), ())), preferred_element_type=F32)


def _rmsnorm_bf16(x, w):
    ms = jnp.mean(x * x, axis=-1, keepdims=True)
    return (x * lax.rsqrt(ms + NORM_EPS) * w).astype(BF16)


def _norm_proj_kernel(*refs, n_out):
    x_ref, nw_ref = refs[0], refs[1]
    w_refs = refs[2:2 + n_out]
    o_refs = refs[2 + n_out:2 + 2 * n_out]
    xn = _rmsnorm_bf16(x_ref[...], nw_ref[...])
    for w_ref, o_ref in zip(w_refs, o_refs):
        o_ref[...] = _dot(xn, w_ref[...]).astype(o_ref.dtype)


def _norm_proj(x2d, norm_w, weights, out_dtypes):
    n, d = x2d.shape
    tm = min(WIDE_TOKEN_TILE, n)
    row = lambda i: (i, 0)
    const = lambda i: (0, 0)
    in_specs = [pl.BlockSpec((tm, d), row), pl.BlockSpec((1, d), const)]
    blk = _nbytes((tm, d), F32)
    out_specs, out_shape = [], []
    for w, dt in zip(weights, out_dtypes):
        in_specs.append(pl.BlockSpec(w.shape, const, pipeline_mode=pl.Buffered(1)))
        out_specs.append(pl.BlockSpec((tm, w.shape[1]), row))
        out_shape.append(jax.ShapeDtypeStruct((n, w.shape[1]), dt))
        blk += _nbytes(w.shape, w.dtype) + _nbytes((tm, w.shape[1]), dt)
    return pl.pallas_call(
        functools.partial(_norm_proj_kernel, n_out=len(weights)), grid=(n // tm,),
        in_specs=in_specs, out_specs=out_specs, out_shape=out_shape,
        compiler_params=_params(("parallel",), blk), name="norm_proj")(
            x2d, norm_w.reshape(1, d), *weights)


def _ret_proj_kernel(x_ref, nw_ref, cr_ref, sr_ref, cb_ref, sb_ref, crt_ref, srt_ref, cbt_ref, sbt_ref,
                     wq_ref, wkt_ref, wv_ref, wg_ref, q_ref, kt_ref, v_ref, g_ref):
    xn = _rmsnorm_bf16(x_ref[...], nw_ref[...])
    half = RET_QK_DIM // 2
    accq = _dot(xn, wq_ref[...])
    cos = cb_ref[...] * cr_ref[...] - sb_ref[...] * sr_ref[...]
    sin = sb_ref[...] * cr_ref[...] + cb_ref[...] * sr_ref[...]
    for h in range(RET_HEADS):
        x1 = accq[:, h * RET_QK_DIM:h * RET_QK_DIM + half]
        x2 = accq[:, h * RET_QK_DIM + half:(h + 1) * RET_QK_DIM]
        q_ref[:, h * RET_QK_DIM:h * RET_QK_DIM + half] = (x1 * cos - x2 * sin).astype(q_ref.dtype)
        q_ref[:, h * RET_QK_DIM + half:(h + 1) * RET_QK_DIM] = (x2 * cos + x1 * sin).astype(q_ref.dtype)
    acck = _dot_nt(wkt_ref[...], xn)
    reps = crt_ref.shape[1] // LANES
    cbt = jnp.concatenate([cbt_ref[...]] * reps, axis=1)
    sbt = jnp.concatenate([sbt_ref[...]] * reps, axis=1)
    cost = cbt * crt_ref[...] - sbt * srt_ref[...]
    sint = sbt * crt_ref[...] + cbt * srt_ref[...]
    scale = RET_QK_DIM ** -0.5
    for h in range(RET_HEADS):
        x1 = acck[h * RET_QK_DIM:h * RET_QK_DIM + half, :]
        x2 = acck[h * RET_QK_DIM + half:(h + 1) * RET_QK_DIM, :]
        kt_ref[h * RET_QK_DIM:h * RET_QK_DIM + half, :] = ((x1 * cost - x2 * sint) * scale).astype(kt_ref.dtype)
        kt_ref[h * RET_QK_DIM + half:(h + 1) * RET_QK_DIM, :] = ((x2 * cost + x1 * sint) * scale).astype(kt_ref.dtype)
    v_ref[...] = _dot(xn, wv_ref[...]).astype(v_ref.dtype)
    g_ref[...] = _dot(xn, wg_ref[...]).astype(g_ref.dtype)


def _ret_proj(x2d, norm_w, angle, w_q, w_kt, w_v, w_g, seq_len):
    n, d = x2d.shape
    tm = min(TOKEN_TILE, seq_len)
    tiles_per_seq = seq_len // tm
    half = angle.shape[0]
    qk_w = w_q.shape[1]
    row = lambda i: (i, 0)
    col = lambda i: (0, i)
    const = lambda i: (0, 0)
    th_r = jnp.arange(tm, dtype=F32)[:, None] * angle[None, :]
    th_b = (jnp.arange(tiles_per_seq, dtype=F32) * tm)[:, None] * angle[None, :]
    cos_r, sin_r = jnp.cos(th_r), jnp.sin(th_r)
    cos_b, sin_b = jnp.cos(th_b)[:, None, :], jnp.sin(th_b)[:, None, :]
    cos_bt = jnp.broadcast_to(jnp.cos(th_b)[:, :, None], (tiles_per_seq, half, LANES))
    sin_bt = jnp.broadcast_to(jnp.sin(th_b)[:, :, None], (tiles_per_seq, half, LANES))
    base = lambda i: (i % tiles_per_seq, 0, 0)
    blk = (_nbytes((tm, d), F32) + 4 * _nbytes((tm, half), F32) + 2 * _nbytes((half, LANES), F32)
           + 2 * _nbytes(w_q.shape, BF16)
           + 2 * _nbytes(w_v.shape, BF16) + 2 * _nbytes((tm, qk_w), BF16) + 2 * _nbytes((tm, D_INNER), BF16))
    return pl.pallas_call(
        _ret_proj_kernel, grid=(n // tm,),
        in_specs=[pl.BlockSpec((tm, d), row), pl.BlockSpec((1, d), const),
                  pl.BlockSpec((tm, half), const), pl.BlockSpec((tm, half), const),
                  pl.BlockSpec((None, 1, half), base), pl.BlockSpec((None, 1, half), base),
                  pl.BlockSpec((half, tm), const), pl.BlockSpec((half, tm), const),
                  pl.BlockSpec((None, half, LANES), base), pl.BlockSpec((None, half, LANES), base),
                  pl.BlockSpec(w_q.shape, const), pl.BlockSpec(w_kt.shape, const),
                  pl.BlockSpec(w_v.shape, const), pl.BlockSpec(w_g.shape, const)],
        out_specs=[pl.BlockSpec((tm, qk_w), row), pl.BlockSpec((qk_w, tm), col),
                   pl.BlockSpec((tm, D_INNER), row), pl.BlockSpec((tm, D_INNER), row)],
        out_shape=[jax.ShapeDtypeStruct((n, qk_w), BF16), jax.ShapeDtypeStruct((qk_w, n), BF16),
                   jax.ShapeDtypeStruct((n, D_INNER), BF16), jax.ShapeDtypeStruct((n, D_INNER), BF16)],
        compiler_params=_params(("parallel",), blk), name="ret_proj")(
            x2d, norm_w.reshape(1, d), cos_r, sin_r, cos_b, sin_b, cos_r.T, sin_r.T, cos_bt, sin_bt,
            w_q, w_kt, w_v, w_g)


def _s5_taps_kernel(r_ref, a_ref, k_ref):
    k_ref[...] = jnp.dot(r_ref[...], a_ref[...], preferred_element_type=F32,
                         precision=lax.Precision.HIGHEST)


def _s5_taps(rcat, apcat):
    g, m, kk = rcat.shape
    t = apcat.shape[2]
    return pl.pallas_call(
        _s5_taps_kernel, grid=(g,),
        in_specs=[pl.BlockSpec((None, m, kk), lambda i: (i, 0, 0)),
                  pl.BlockSpec((None, kk, t), lambda i: (i, 0, 0))],
        out_specs=pl.BlockSpec((None, m, t), lambda i: (i, 0, 0)),
        out_shape=jax.ShapeDtypeStruct((g, m, t), F32),
        compiler_params=_params(("parallel",), _nbytes((m, kk + t), F32) + _nbytes((kk, t), F32)),
        name="s5_taps")(rcat, apcat)


def _s5_core_kernel(u_ref, k_ref, f_ref, bt_ref, a_ref, ct_ref, sc_ref, y_ref, w_ref, *, n_chunks):
    ts = S5_CHUNK
    h = S5_GROUP
    half = S5_STATE
    rows = u_ref.shape[0]
    u = u_ref[...]

    f1 = f_ref[0]
    f2 = f_ref[1]
    pmat = jnp.concatenate(
        [(f1 * bt_ref[0, hi:hi + 1, :] + f2 * bt_ref[1, hi:hi + 1, :]).astype(BF16) for hi in range(h)], axis=0)
    e = _dot(u, pmat)

    row_idx = lax.broadcasted_iota(jnp.int32, (n_chunks, 2 * half), 0)
    hprev = []
    for b in range(rows // n_chunks):
        acc = e[b * n_chunks:(b + 1) * n_chunks, :]
        acc_sw = pltpu.roll(acc, half, 1)
        d = 1
        lvl = 0
        while d < n_chunks:
            c1 = sc_ref[2 * lvl:2 * lvl + 1, :]
            c2 = sc_ref[2 * lvl + 1:2 * lvl + 2, :]
            sh = jnp.where(row_idx >= d, pltpu.roll(acc, d, 0), 0.0)
            sh_sw = jnp.where(row_idx >= d, pltpu.roll(acc_sw, d, 0), 0.0)
            acc = acc + c1 * sh + c2 * sh_sw
            acc_sw = acc_sw + c1 * sh_sw - c2 * sh
            d *= 2
            lvl += 1
        hprev.append(jnp.where(row_idx >= 1, pltpu.roll(acc, 1, 0), 0.0))
    hp = jnp.concatenate(hprev, axis=0).astype(BF16)

    s_idx = lax.broadcasted_iota(jnp.int32, (ts, LANES), 0)
    l_idx = lax.broadcasted_iota(jnp.int32, (ts, LANES), 1)
    keep = (l_idx & (ts - 1)) >= s_idx
    first = lax.broadcasted_iota(jnp.int32, (2 * half, LANES), 1) < ts
    slabs_per_tile = MXU_DIM // LANES
    kdim = h * ts
    for n in range(h * ts // MXU_DIM):
        for j in range(slabs_per_tile):
            m = n * slabs_per_tile + j
            cols = slice(j * LANES, (j + 1) * LANES)
            for hi in range(h):
                kb = jnp.broadcast_to(k_ref[hi * (h // 2) + m:hi * (h // 2) + m + 1, :], (ts, LANES))
                tz = pltpu.roll(kb, 0, 1, stride=1, stride_axis=0)
                w_ref[n, hi * ts:(hi + 1) * ts, cols] = jnp.where(keep, tz, 0.0).astype(BF16)
            k1 = jnp.where(first, jnp.broadcast_to(ct_ref[0, :, 2 * m:2 * m + 1], (2 * half, LANES)),
                           jnp.broadcast_to(ct_ref[0, :, 2 * m + 1:2 * m + 2], (2 * half, LANES)))
            k2 = jnp.where(first, jnp.broadcast_to(ct_ref[1, :, 2 * m:2 * m + 1], (2 * half, LANES)),
                           jnp.broadcast_to(ct_ref[1, :, 2 * m + 1:2 * m + 2], (2 * half, LANES)))
            w_ref[n, kdim:kdim + 2 * half, cols] = (k1 * a_ref[0] + k2 * a_ref[1]).astype(BF16)
        y = _dot(u, w_ref[n, 0:kdim, :]) + _dot(hp, w_ref[n, kdim:kdim + 2 * half, :])
        y_ref[:, n * MXU_DIM:(n + 1) * MXU_DIM] = y.astype(y_ref.dtype)


def _s5_core(u2, taps2, ftab, bttab, atab, cttab, sc, n_chunks):
    rows = u2.shape[0]
    g = taps2.shape[0]
    w = S5_GROUP * S5_CHUNK
    tabs = [taps2, ftab, bttab, atab, cttab, sc]
    blk = 2 * _nbytes((rows, w), BF16) + sum(_nbytes(t.shape[1:], F32) for t in tabs)
    wshape = (w // MXU_DIM, w + 2 * S5_STATE, MXU_DIM)

    def tab_spec(t):
        nd = t.ndim - 1
        return pl.BlockSpec((None,) + t.shape[1:], lambda i: (i,) + (0,) * nd)

    return pl.pallas_call(
        functools.partial(_s5_core_kernel, n_chunks=n_chunks), grid=(g,),
        in_specs=[pl.BlockSpec((rows, w), lambda i: (0, i))] + [tab_spec(t) for t in tabs],
        out_specs=pl.BlockSpec((rows, w), lambda i: (0, i)),
        out_shape=jax.ShapeDtypeStruct((rows, g * w), BF16),
        scratch_shapes=[pltpu.VMEM(wshape, BF16)],
        compiler_params=_params(("parallel",), blk, _nbytes(wshape, BF16)),
        name="s5_core")(u2, *tabs)


def _s5_prepare(lam_re, lam_im, log_dt, b_re, b_im, c_re, c_im, n_chunks):
    ts = S5_CHUNK
    assert 2 * ts == LANES and 2 * S5_STATE == LANES
    dt = jnp.exp(log_dt.astype(F32))[:, None]
    lr = jnp.minimum(lam_re.astype(F32), -1e-4)
    li = lam_im.astype(F32)
    mag = jnp.exp(lr * dt)
    ab_re = mag * jnp.cos(li * dt)
    ab_im = mag * jnp.sin(li * dt)
    den = lr * lr + li * li
    nr = ab_re - 1.0
    coef_re = (nr * lr + ab_im * li) / den
    coef_im = (ab_im * lr - nr * li) / den
    br = b_re.astype(F32)
    bi = b_im.astype(F32)
    bb_re = coef_re[..., None] * br - coef_im[..., None] * bi
    bb_im = coef_re[..., None] * bi + coef_im[..., None] * br
    cr = c_re.astype(F32)
    ci = c_im.astype(F32)
    g = lr.shape[0]
    j = jnp.arange(ts + 1, dtype=F32)
    pmag = jnp.exp((lr * dt)[..., None] * j)
    ang = (li * dt)[..., None] * j
    pw_re = pmag * jnp.cos(ang)
    pw_im = pmag * jnp.sin(ang)
    bbt_re = bb_re.transpose(0, 2, 1)
    bbt_im = bb_im.transpose(0, 2, 1)
    r_re = cr[:, None] * bbt_re[:, :, None] - ci[:, None] * bbt_im[:, :, None]
    r_im = cr[:, None] * bbt_im[:, :, None] + ci[:, None] * bbt_re[:, :, None]
    rcat = jnp.concatenate([r_re, -r_im], axis=-1).reshape(g, S5_GROUP * S5_GROUP, 2 * S5_STATE)
    apcat = jnp.concatenate([pw_re[..., :ts], pw_im[..., :ts]], axis=1)
    fr = pw_re[..., ts - 1::-1].transpose(0, 2, 1)
    fi = pw_im[..., ts - 1::-1].transpose(0, 2, 1)
    ftab = jnp.stack([jnp.concatenate([fr, fr], -1), jnp.concatenate([-fi, fi], -1)], axis=1)
    bttab = jnp.stack([jnp.concatenate([bbt_re, bbt_im], -1), jnp.concatenate([bbt_im, bbt_re], -1)], axis=1)
    wr = pw_re[..., 1:]
    wi = pw_im[..., 1:]
    a1 = jnp.concatenate([wr, -wi], axis=1)
    a2 = jnp.concatenate([-wi, -wr], axis=1)
    atab = jnp.stack([jnp.concatenate([a1, a1], -1), jnp.concatenate([a2, a2], -1)], axis=1)
    crt = cr.transpose(0, 2, 1)
    cit = ci.transpose(0, 2, 1)
    cttab = jnp.stack([jnp.concatenate([crt, crt], 1), jnp.concatenate([cit, cit], 1)], axis=1)
    rows = []
    d = 1
    while d < n_chunks:
        ex = float(ts * d)
        m = jnp.exp(lr * dt * ex)
        a_r = m * jnp.cos(li * dt * ex)
        a_i = m * jnp.sin(li * dt * ex)
        rows.append(jnp.concatenate([a_r, a_r], axis=-1))
        rows.append(jnp.concatenate([-a_i, a_i], axis=-1))
        d *= 2
    while len(rows) % SUBLANES or not rows:
        rows.append(jnp.zeros((g, 2 * S5_STATE), F32))
    sc = jnp.stack(rows, axis=1)
    return rcat, apcat, ftab, bttab, atab, cttab, sc


def _ssd_kernel(xbc_ref, z_ref, dt_ref, shift_ref, tri_ref, exp_ref, cw_ref, cb_ref, dtb_ref, a_ref,
                dsk_ref, nw_ref, y_ref, xcat_ref, st_ref, yg_ref):
    t = CHUNK
    c = pl.program_id(0)

    @pl.when(c == 0)
    def _():
        xcat_ref[:, 0:t, :] = jnp.zeros((xcat_ref.shape[0], t, SSD_CONV_DIM), BF16)
        st_ref[...] = jnp.zeros_like(st_ref)

    @pl.when(c != 0)
    def _():
        xcat_ref[:, 0:t, :] = xcat_ref[:, t:2 * t, :]

    for b in range(xbc_ref.shape[0]):
        _ssd_chunk(xbc_ref.at[b], z_ref.at[b], dt_ref.at[b], shift_ref, tri_ref, exp_ref, cw_ref, cb_ref,
                   dtb_ref, a_ref, dsk_ref, nw_ref, y_ref.at[b], xcat_ref.at[b],
                   st_ref.at[pl.ds(b * SSD_GROUPS, SSD_GROUPS)], yg_ref.at[b])


def _ssd_chunk(xbc_ref, z_ref, dt_ref, shift_ref, tri_ref, exp_ref, cw_ref, cb_ref, dtb_ref, a_ref,
               dsk_ref, nw_ref, y_ref, xcat_ref, st_ref, yg_ref):
    t = CHUNK
    xcat_ref[t:2 * t, :] = xbc_ref[...]
    xsh = _dot(shift_ref[...], xcat_ref[...])
    acc = cb_ref[...] + cw_ref[SSD_CONV - 1:SSD_CONV, :] * xbc_ref[...].astype(F32)
    for k in range(SSD_CONV - 1):
        acc = acc + cw_ref[k:k + 1, :] * xsh[k * t:(k + 1) * t, :]
    xc = _silu(acc)
    gn = SSD_GROUPS * SSD_STATE
    xs = xc[:, :SSD_WIDTH]
    bm = xc[:, SSD_WIDTH:SSD_WIDTH + gn]
    cm = xc[:, SSD_WIDTH + gn:]

    xr = dt_ref[...] + dtb_ref[...]
    dt = jnp.maximum(xr, 0.0) + jnp.log(1.0 + jnp.exp(-jnp.abs(xr)))
    da = dt * a_ref[...]
    cs3 = _dot(tri_ref[...], jnp.concatenate(_split3(da), axis=1))
    a_cs = cs3[:, :LANES] + cs3[:, LANES:2 * LANES] + cs3[:, 2 * LANES:]
    expand = exp_ref[...]
    dt_e = _dot(jnp.concatenate(_split3(dt)[:2], axis=1), expand)
    acs_e = _dot(jnp.concatenate(_split3(a_cs)[:2], axis=1), expand)
    last_e = acs_e[t - 1:t, :]
    xdt = xs * dt_e
    xdt_b = xdt.astype(BF16)
    xd_b = (xdt * jnp.exp(last_e - acs_e)).astype(BF16)
    exp_acs = jnp.exp(acs_e)
    chunk_decay = jnp.exp(last_e)
    a_cs_t = a_cs.T
    r_i = lax.broadcasted_iota(jnp.int32, (t, t), 0)
    c_i = lax.broadcasted_iota(jnp.int32, (t, t), 1)
    causal = r_i >= c_i
    lane = lax.broadcasted_iota(jnp.int32, (t, LANES), 1)
    gw = SSD_REP * SSD_HEAD_DIM
    for g in range(SSD_GROUPS):
        bg = bm[:, g * SSD_STATE:(g + 1) * SSD_STATE]
        cg = cm[:, g * SSD_STATE:(g + 1) * SSD_STATE].astype(BF16)
        gmat = _dot_nt(cg, bg.astype(BF16))
        s_prev = st_ref[g]
        y_off = _dot(cg, s_prev.astype(BF16)) * exp_acs[:, g * gw:(g + 1) * gw]
        st_ref[g] = (s_prev * chunk_decay[:, g * gw:(g + 1) * gw]
                     + _dot(bg.T.astype(BF16), xd_b[:, g * gw:(g + 1) * gw]))
        for pr in range(SSD_REP // 2):
            ms = []
            for hh in range(2):
                hd = g * SSD_REP + 2 * pr + hh
                seg = a_cs[:, hd:hd + 1] - a_cs_t[hd:hd + 1, :]
                ms.append((gmat * jnp.exp(jnp.where(causal, seg, -jnp.inf))).astype(BF16))
            lo = g * gw + pr * LANES
            x2 = xdt_b[:, lo:lo + LANES]
            rhs = jnp.concatenate([jnp.where(lane < SSD_HEAD_DIM, x2, jnp.zeros_like(x2)),
                                   jnp.where(lane >= SSD_HEAD_DIM, x2, jnp.zeros_like(x2))], axis=0)
            y_d = _dot(jnp.concatenate(ms, axis=1), rhs)
            y = y_d + y_off[:, pr * LANES:(pr + 1) * LANES] + xs[:, lo:lo + LANES] * dsk_ref[:, lo:lo + LANES]
            yg_ref[:, lo:lo + LANES] = y * _silu(z_ref[:, lo:lo + LANES].astype(F32))
    for g in range(SSD_GROUPS):
        yg = yg_ref[:, g * gw:(g + 1) * gw]
        ms_ = jnp.mean(yg * yg, axis=-1, keepdims=True)
        y_ref[:, g * gw:(g + 1) * gw] = (yg * lax.rsqrt(ms_ + NORM_EPS)
                                         * nw_ref[:, g * gw:(g + 1) * gw]).astype(y_ref.dtype)


def _ssd(xbc, z, dt, conv_w, conv_b, dt_bias, a_neg, d_e, norm_w, batch, n_chunks):
    t = CHUNK
    n = xbc.shape[0]
    seq_len = n // batch
    row = lambda c: (0, c, 0)
    const = lambda c: (0, 0)
    xbc = xbc.reshape(batch, seq_len, SSD_CONV_DIM)
    z = z.reshape(batch, seq_len, z.shape[1])
    dt = dt.reshape(batch, seq_len, LANES)
    rr = jnp.arange((SSD_CONV - 1) * t)
    src = t + rr % t - (SSD_CONV - 1) + rr // t
    shift = (jnp.arange(2 * t)[None, :] == src[:, None]).astype(BF16)
    tri = (jnp.arange(t)[:, None] >= jnp.arange(t)[None, :]).astype(BF16)
    expand = (jnp.arange(SSD_WIDTH)[None, :] // SSD_HEAD_DIM
              == (jnp.arange(2 * LANES) % LANES)[:, None]).astype(BF16)
    blk = (batch * (_nbytes((t, SSD_CONV_DIM), BF16) + 2 * _nbytes((t, SSD_WIDTH), BF16)
                    + _nbytes((t, LANES), F32))
           + _nbytes(shift.shape, BF16) + _nbytes(tri.shape, BF16) + _nbytes(expand.shape, BF16)
           + 6 * _nbytes((SUBLANES, SSD_CONV_DIM), F32))
    st_shape = (batch * SSD_GROUPS, SSD_STATE, SSD_REP * SSD_HEAD_DIM)
    scratch = (_nbytes((batch, 2 * t, SSD_CONV_DIM), BF16) + _nbytes(st_shape, F32)
               + _nbytes((batch, t, SSD_WIDTH), F32))
    out = pl.pallas_call(
        _ssd_kernel, grid=(n_chunks,),
        in_specs=[pl.BlockSpec((batch, t, SSD_CONV_DIM), row),
                  pl.BlockSpec((batch, t, SSD_WIDTH), lambda c: (0, c, 1)),
                  pl.BlockSpec((batch, t, LANES), row),
                  pl.BlockSpec(shift.shape, const), pl.BlockSpec(tri.shape, const),
                  pl.BlockSpec(expand.shape, const),
                  pl.BlockSpec(conv_w.shape, const), pl.BlockSpec(conv_b.shape, const),
                  pl.BlockSpec(dt_bias.shape, const), pl.BlockSpec(a_neg.shape, const),
                  pl.BlockSpec(d_e.shape, const), pl.BlockSpec(norm_w.shape, const)],
        out_specs=pl.BlockSpec((batch, t, SSD_WIDTH), row),
        out_shape=jax.ShapeDtypeStruct((batch, seq_len, SSD_WIDTH), BF16),
        scratch_shapes=[pltpu.VMEM((batch, 2 * t, SSD_CONV_DIM), BF16),
                        pltpu.VMEM(st_shape, F32),
                        pltpu.VMEM((batch, t, SSD_WIDTH), F32)],
        compiler_params=_params(("arbitrary",), blk, scratch),
        name="ssd")(xbc, z, dt, shift, tri, expand, conv_w, conv_b, dt_bias, a_neg, d_e, norm_w)
    return out.reshape(n, SSD_WIDTH)


def _post0_kernel(ys_ref, u_ref, z_ref, yb_ref, x_ref, d_ref, gw_ref, gb_ref, wa_ref, wb_ref, o_ref):
    u = u_ref[...].astype(F32)
    v = ys_ref[...].astype(F32) + d_ref[...] * u
    y = 0.5 * v * (1.0 + jnp.tanh(math.sqrt(2.0 / math.pi) * (v + 0.044715 * (v * v * v))))
    gl = _dot(y.astype(BF16), gw_ref[...]) + gb_ref[...]
    ya = y * _sigmoid(gl) * _silu(z_ref[...].astype(F32))
    out = _dot(ya.astype(BF16), wa_ref[...]) + _dot(yb_ref[...].astype(BF16), wb_ref[...])
    o_ref[...] = x_ref[...] + out


def _post0(ys, u, z, yb, x2d, s5_d, glu_w, glu_b, w_a, w_b):
    n, d = x2d.shape
    tm = min(WIDE_TOKEN_TILE, n)
    row = lambda i: (i, 0)
    const = lambda i: (0, 0)
    once = pl.Buffered(1)
    blk = (3 * _nbytes((tm, d), BF16) + 3 * _nbytes((tm, d), F32) + 3 * _nbytes((d, d), BF16)
           + 2 * _nbytes((SUBLANES, d), F32))
    return pl.pallas_call(
        _post0_kernel, grid=(n // tm,),
        in_specs=[pl.BlockSpec((tm, d), row), pl.BlockSpec((tm, d), row), pl.BlockSpec((tm, d), row),
                  pl.BlockSpec((tm, d), row), pl.BlockSpec((tm, d), row),
                  pl.BlockSpec((1, d), const), pl.BlockSpec(glu_w.shape, const, pipeline_mode=once),
                  pl.BlockSpec((1, d), const),
                  pl.BlockSpec(w_a.shape, const, pipeline_mode=once),
                  pl.BlockSpec(w_b.shape, const, pipeline_mode=once)],
        out_specs=pl.BlockSpec((tm, d), row),
        out_shape=jax.ShapeDtypeStruct((n, d), F32),
        compiler_params=_params(("parallel",), blk), name="post0")(
            ys, u, z, yb, x2d, s5_d, glu_w, glu_b, w_a, w_b)


def _ret_kernel(*refs, n_seq, final_norm):
    q_ref = refs[0]
    kt_refs = refs[1:1 + n_seq]
    (v_ref, g_ref, h_ref, dmat_ref, qd_ref, kdt_ref, cd_ref, gw_ref, gb_ref, wo_ref, fw_ref,
     o_ref, st_ref, y_ref) = refs[1 + n_seq:]

    @pl.when(pl.program_id(0) == 0)
    def _():
        st_ref[...] = jnp.zeros_like(st_ref)

    for b in range(n_seq):
        _ret_chunk(q_ref.at[b], kt_refs[b], v_ref.at[b], g_ref.at[b], h_ref.at[b], dmat_ref, qd_ref, kdt_ref,
                   cd_ref, gw_ref, gb_ref, wo_ref, fw_ref, o_ref.at[b],
                   st_ref.at[pl.ds(b * RET_HEADS, RET_HEADS)], y_ref.at[b], final_norm=final_norm)


def _ret_chunk(q_ref, kt_ref, v_ref, g_ref, h_ref, dmat_ref, qd_ref, kdt_ref, cd_ref, gw_ref, gb_ref,
               wo_ref, fw_ref, o_ref, st_ref, y_ref, *, final_norm):
    for h in range(RET_HEADS):
        qh = q_ref[:, h * RET_QK_DIM:(h + 1) * RET_QK_DIM]
        kth = kt_ref[h * RET_QK_DIM:(h + 1) * RET_QK_DIM, :]
        vh = v_ref[:, h * RET_V_DIM:(h + 1) * RET_V_DIM]
        scores = _dot(qh, kth) * dmat_ref[h]
        inner = _dot(scores.astype(BF16), vh)
        s_prev = st_ref[h]
        qdec = (qh.astype(F32) * qd_ref[h]).astype(BF16)
        cross = _dot(qdec, s_prev.astype(BF16))
        kdec = (kth.astype(F32) * kdt_ref[h]).astype(BF16)
        st_ref[h] = s_prev * cd_ref[h] + _dot(kdec, vh)
        o = inner + cross
        mu = jnp.mean(o, axis=-1, keepdims=True)
        oc = o - mu
        var = jnp.mean(oc * oc, axis=-1, keepdims=True)
        sl = slice(h * RET_V_DIM, (h + 1) * RET_V_DIM)
        on = oc * lax.rsqrt(var + NORM_EPS) * gw_ref[:, sl] + gb_ref[:, sl]
        y_ref[:, sl] = (_silu(g_ref[:, sl].astype(F32)) * on).astype(y_ref.dtype)
    hn = h_ref[...] + _dot(y_ref[...], wo_ref[...])
    if final_norm:
        ms = jnp.mean(hn * hn, axis=-1, keepdims=True)
        hn = hn * lax.rsqrt(ms + NORM_EPS) * fw_ref[...]
    o_ref[...] = hn


def _retention(q, kt, v, g, h2d, dmat, qd, kdt, cd, gn_w, gn_b, w_out, final_w, batch, seq_len, final_norm):
    rt = dmat.shape[1]
    n_chunks = seq_len // rt
    n, d = h2d.shape
    row = lambda c: (0, c, 0)
    c2 = lambda c: (0, 0)
    c3 = lambda c: (0, 0, 0)
    seq3 = lambda a: a.reshape(batch, seq_len, a.shape[1])
    blk = (batch * (2 * _nbytes((rt, D_MODEL), BF16) + 2 * _nbytes((rt, D_INNER), BF16)
                    + 2 * _nbytes((rt, d), F32))
           + _nbytes(dmat.shape, F32) + 2 * _nbytes(qd.shape, F32) + _nbytes(cd.shape, F32)
           + _nbytes(w_out.shape, BF16) + 3 * _nbytes((SUBLANES, D_INNER), F32))
    st_shape = (batch * RET_HEADS, RET_QK_DIM, RET_V_DIM)
    scratch = _nbytes(st_shape, F32) + _nbytes((batch, rt, D_INNER), BF16)
    kt_specs = [pl.BlockSpec((D_MODEL, rt), functools.partial(lambda c, b: (0, b * n_chunks + c), b=b))
                for b in range(batch)]
    out = pl.pallas_call(
        functools.partial(_ret_kernel, n_seq=batch, final_norm=final_norm), grid=(n_chunks,),
        in_specs=[pl.BlockSpec((batch, rt, D_MODEL), row)] + kt_specs
                 + [pl.BlockSpec((batch, rt, D_INNER), row), pl.BlockSpec((batch, rt, D_INNER), row),
                    pl.BlockSpec((batch, rt, d), row),
                    pl.BlockSpec(dmat.shape, c3), pl.BlockSpec(qd.shape, c3), pl.BlockSpec(kdt.shape, c3),
                    pl.BlockSpec(cd.shape, c3), pl.BlockSpec((1, D_INNER), c2), pl.BlockSpec((1, D_INNER), c2),
                    pl.BlockSpec(w_out.shape, c2), pl.BlockSpec((1, d), c2)],
        out_specs=pl.BlockSpec((batch, rt, d), row),
        out_shape=jax.ShapeDtypeStruct((batch, seq_len, d), F32),
        scratch_shapes=[pltpu.VMEM(st_shape, F32), pltpu.VMEM((batch, rt, D_INNER), BF16)],
        compiler_params=_params(("arbitrary",), blk, scratch),
        name="retention")(seq3(q), *([kt] * batch), seq3(v), seq3(g), seq3(h2d), dmat, qd, kdt, cd,
                          gn_w, gn_b, w_out, final_w)
    return out.reshape(n, d)


def _s5_ssd_layer(h2d, norm_w, batch, seq_len, w_in, lam_re, lam_im, log_dt, b_re, b_im, c_re, c_im,
                  s5_d, glu_w, glu_b, conv_w, conv_b, dt_bias, a_log, ssd_d, ssd_norm_w, w_out):
    n = batch * seq_len
    o1 = D_INNER
    o2 = o1 + S5_WIDTH
    o3 = o2 + SSD_CONV_DIM
    w_z = w_in[:, :o1].astype(BF16)
    w_u = w_in[:, o1:o2].astype(BF16)
    w_x = w_in[:, o2:o3].astype(BF16)
    w_dt = jnp.pad(w_in[:, o3:], ((0, 0), (0, LANES - SSD_HEADS))).astype(BF16)
    z, u, xbc, dt = _norm_proj(h2d, norm_w, [w_z, w_u, w_x, w_dt], [BF16, BF16, BF16, F32])

    ts = S5_CHUNK
    nc_s = seq_len // ts
    rcat, apcat, ftab, bttab, atab, cttab, sc = _s5_prepare(lam_re, lam_im, log_dt, b_re, b_im, c_re, c_im, nc_s)
    taps2 = _s5_taps(rcat, apcat).reshape(S5_GROUPS, S5_GROUP * S5_GROUP // 2, 2 * ts)
    u2 = u.reshape(batch * nc_s, ts, S5_WIDTH).transpose(0, 2, 1).reshape(batch * nc_s, S5_WIDTH * ts)
    y2 = _s5_core(u2, taps2, ftab, bttab, atab, cttab, sc, nc_s)
    ys = y2.reshape(batch * nc_s, S5_WIDTH, ts).transpose(0, 2, 1).reshape(n, S5_WIDTH)

    pad_h = LANES - SSD_HEADS
    cw = jnp.pad(conv_w.astype(F32), ((0, SUBLANES - SSD_CONV), (0, 0)))
    cb = conv_b.astype(F32).reshape(1, SSD_CONV_DIM)
    dtb = jnp.pad(dt_bias.astype(F32), (0, pad_h)).reshape(1, LANES)
    a_neg = jnp.pad(-jnp.exp(a_log.astype(F32)), (0, pad_h)).reshape(1, LANES)
    d_e = jnp.repeat(ssd_d.astype(F32), SSD_HEAD_DIM).reshape(1, SSD_WIDTH)
    nw = ssd_norm_w.astype(F32).reshape(1, SSD_WIDTH)
    yb = _ssd(xbc, z, dt, cw, cb, dtb, a_neg, d_e, nw, batch, seq_len // CHUNK)

    return _post0(ys, u, z, yb, h2d, s5_d.astype(F32).reshape(1, S5_WIDTH), glu_w.astype(BF16),
                  glu_b.astype(F32).reshape(1, S5_WIDTH), w_out[:S5_WIDTH].astype(BF16),
                  w_out[S5_WIDTH:].astype(BF16))


def _retention_layer(h2d, norm_w, batch, seq_len, w_in, gn_w, gn_b, w_out, final_w, final_norm):
    qk_w = RET_HEADS * RET_QK_DIM
    half = RET_QK_DIM // 2

    def deinterleave(w):
        return w.reshape(D_MODEL, RET_HEADS, half, 2).transpose(0, 1, 3, 2).reshape(D_MODEL, qk_w)

    w_q = deinterleave(w_in[:, :qk_w]).astype(BF16)
    w_kt = deinterleave(w_in[:, qk_w:2 * qk_w]).T.astype(BF16)
    w_v = w_in[:, 2 * qk_w:2 * qk_w + D_INNER].astype(BF16)
    w_g = w_in[:, 2 * qk_w + D_INNER:].astype(BF16)
    angle = 1.0 / (ROPE_BASE ** jnp.linspace(0.0, 1.0, half, dtype=F32))
    q, kt, v, g = _ret_proj(h2d, norm_w, angle, w_q, w_kt, w_v, w_g, seq_len)

    rt = min(RET_CHUNK, seq_len)
    log_gamma = jnp.log(1.0 - 2.0 ** (-5.0 - jnp.arange(RET_HEADS, dtype=F32)))
    p = jnp.arange(rt, dtype=F32)
    rel = p[:, None] - p[None, :]
    dmat = jnp.where(rel >= 0, jnp.exp(log_gamma[:, None, None] * jnp.maximum(rel, 0.0)), 0.0)
    qd = jnp.broadcast_to(jnp.exp(log_gamma[:, None] * (p[None, :] + 1.0))[..., None],
                          (RET_HEADS, rt, RET_QK_DIM))
    kdt = jnp.broadcast_to(jnp.exp(log_gamma[:, None] * (rt - 1.0 - p)[None, :])[:, None, :],
                           (RET_HEADS, RET_QK_DIM, rt))
    cd = jnp.broadcast_to(jnp.exp(log_gamma * rt)[:, None, None], (RET_HEADS, 1, RET_V_DIM))
    return _retention(q, kt, v, g, h2d, dmat, qd, kdt, cd, gn_w.astype(F32).reshape(1, D_INNER),
                      gn_b.astype(F32).reshape(1, D_INNER), w_out.astype(BF16),
                      final_w.astype(F32).reshape(1, D_MODEL), batch, seq_len, final_norm)


def kernel(x, layer_norm_w, ab_w_in, s5_lam_re, s5_lam_im, s5_log_dt, s5_b_re, s5_b_im, s5_c_re, s5_c_im,
           s5_d, s5_glu_w, s5_glu_b, ssd_conv_w, ssd_conv_b, ssd_dt_bias, ssd_a_log, ssd_d, ssd_norm_w,
           ab_w_out, ret_w_in, ret_gn_w, ret_gn_b, ret_w_out, final_norm_w):
    batch, seq_len, d = x.shape
    depth = layer_norm_w.shape[0]
    assert depth % 2 == 0 and seq_len % CHUNK == 0 and d == D_MODEL
    h = x.astype(F32).reshape(batch * seq_len, d)
    for i in range(depth):
        j = i // 2
        if i % 2 == 0:
            h = _s5_ssd_layer(h, layer_norm_w[i], batch, seq_len, ab_w_in[j], s5_lam_re[j], s5_lam_im[j],
                              s5_log_dt[j], s5_b_re[j], s5_b_im[j], s5_c_re[j], s5_c_im[j], s5_d[j],
                              s5_glu_w[j], s5_glu_b[j], ssd_conv_w[j], ssd_conv_b[j], ssd_dt_bias[j],
                              ssd_a_log[j], ssd_d[j], ssd_norm_w[j], ab_w_out[j])
        else:
            h = _retention_layer(h, layer_norm_w[i], batch, seq_len, ret_w_in[j], ret_gn_w[j], ret_gn_b[j],
                                 ret_w_out[j], final_norm_w, final_norm=(i == depth - 1))
    return h.reshape(batch, seq_len, d).astype(x.dtype)
```
